```python
import jax, jax.numpy as jnp
from jax import lax
import numpy as np

D_MODEL = 4096
BATCH = 4
SEQ = 2048
DEPTH = 4
DEC_BATCH = 8
DEC_SEQ = 4
PAST_LEN = 8192
PAGE_SIZE = 128

HEAD_DIM = 128
FOX_HEADS = 12
FOX_KV_HEADS = 4
FOX_GROUP = FOX_HEADS // FOX_KV_HEADS
SB_HEADS = 12
SB_KV_HEADS = 4
SB_GROUP = SB_HEADS // SB_KV_HEADS
HGRN_HEADS = 8
HGRN_EXPAND = 128
HGRN_VDIM = 128
HGRN_CHUNK = 64
Q_BLOCK = 128
N_EXPERTS = 32
TOP_K = 4
D_EXPERT = D_MODEL // 4
SWIGLU_LIMIT = 7.0
SWIGLU_ALPHA = 1.702
MOE_BLOCK = 128
FOX_FGATE_BIAS = 4.0
NORM_EPS = 1e-6
NEG_BIG = -1e30
N_MOD = 6

FOX_WIDTH = FOX_HEADS * HEAD_DIM
FOX_KV_WIDTH = FOX_KV_HEADS * HEAD_DIM
SB_WIDTH = SB_HEADS * HEAD_DIM
SB_KV_WIDTH = SB_KV_HEADS * HEAD_DIM
HGRN_FDIM = HGRN_HEADS * HGRN_EXPAND
HGRN_WIDTH = HGRN_HEADS * HGRN_VDIM
IN_COLS = (FOX_WIDTH, FOX_KV_WIDTH, FOX_KV_WIDTH, FOX_HEADS,
           SB_WIDTH, SB_KV_WIDTH, SB_KV_WIDTH,
           HGRN_FDIM, HGRN_FDIM, HGRN_WIDTH, HGRN_WIDTH,
           D_MODEL, D_MODEL, D_MODEL)
IN_WIDTH = sum(IN_COLS)
ATT_SCALE = HEAD_DIM ** -0.5

kernel_name = 'fox_stickbreak_hgrn2_moe_adaln_step'


def rmsnorm(x, g):
    xf = x.astype(jnp.float32)
    y = xf * lax.rsqrt(jnp.mean(xf * xf, axis=-1, keepdims=True) + NORM_EPS)
    return (y * g.astype(jnp.float32)).astype(x.dtype)


def split_cols(z, widths):
    out, start = [], 0
    for w in widths:
        out.append(z[..., start:start + w])
        start += w
    return out


def gather_pages(pool, page_table):
    g = pool[page_table]
    return g.reshape(g.shape[0], g.shape[1] * g.shape[2], *g.shape[3:])


def sweep_queries(block_fn, q_args, qpos, kv_args):
    n_q = qpos.shape[0]
    blk = Q_BLOCK if n_q % Q_BLOCK == 0 else n_q
    nb = n_q // blk

    def split(a):
        return jnp.swapaxes(a.reshape(a.shape[0], nb, blk, *a.shape[2:]), 0, 1)

    def one_block(args):
        qs, qp = args
        return block_fn(*qs, qp, *kv_args)

    out = lax.map(one_block, (tuple(split(a) for a in q_args), qpos.reshape(nb, blk)))
    out = jnp.swapaxes(out, 0, 1)
    return out.reshape(out.shape[0], n_q, *out.shape[3:])


def fox_block(q, cq, qpos, k, v, ck, kpos):
    s = jnp.einsum('bqkgd,bskd->bkgqs', q, k, preferred_element_type=jnp.float32) * ATT_SCALE
    bias = jnp.moveaxis(cq, 1, -1)[..., :, None] - jnp.moveaxis(ck, 1, -1)[..., None, :]
    mask = kpos[None, :] <= qpos[:, None]
    p = jax.nn.softmax(jnp.where(mask, s + bias, NEG_BIG), axis=-1)
    o = jnp.einsum('bkgqs,bskd->bqkgd', p.astype(v.dtype), v, preferred_element_type=jnp.float32)
    return o.astype(v.dtype)


def sb_block(q, qpos, k, v, kpos):
    z = jnp.einsum('bqkgd,bskd->bkgqs', q, k, preferred_element_type=jnp.float32) * ATT_SCALE
    mask = kpos[None, :] < qpos[:, None]
    ln = jnp.where(mask, jax.nn.log_sigmoid(-z), 0.0)
    ln_next = jnp.concatenate([ln[..., 1:], jnp.zeros_like(ln[..., :1])], axis=-1)
    after = lax.cumsum(ln_next, axis=ln.ndim - 1, reverse=True)
    a = jnp.where(mask, jnp.exp(jax.nn.log_sigmoid(z) + after), 0.0)
    o = jnp.einsum('bkgqs,bskd->bqkgd', a.astype(v.dtype), v, preferred_element_type=jnp.float32)
    return o.astype(v.dtype)


def hgrn_chunked(q, k, v, g, s0, chunk):
    B, L, H, N = q.shape
    M = v.shape[-1]
    nc = L // chunk
    causal = jnp.tril(jnp.ones((chunk, chunk), bool))[:, :, None]

    def to_chunks(a):
        return jnp.transpose(a.reshape(B, nc, chunk, H, a.shape[-1]), (1, 0, 3, 2, 4))

    def step(S, inp):
        qc, kc, vc, gc = inp
        G = jnp.cumsum(gc, axis=2)
        diff = G[:, :, :, None, :] - G[:, :, None, :, :]
        decay = jnp.where(causal, jnp.exp(jnp.where(causal, diff, 0.0)), 0.0)
        A = jnp.einsum('bhtn,bhsn,bhtsn->bhts', qc, kc, decay)
        o = jnp.einsum('bhts,bhsm->bhtm', A, vc) + jnp.einsum('bhtn,bhnm->bhtm', qc * jnp.exp(G), S)
        G_end = G[:, :, -1, :]
        S = jnp.exp(G_end)[..., None] * S + jnp.einsum('bhsn,bhsm->bhnm', kc * jnp.exp(G_end[:, :, None, :] - G), vc)
        return S, o

    s_fin, o = lax.scan(step, s0, tuple(to_chunks(a) for a in (q, k, v, g)))
    o = jnp.transpose(o, (1, 0, 3, 2, 4)).reshape(B, L, H, M)
    return o, s_fin


def token_mixers(u, past, w_in_l, fgate_b, lb, gnorm_g, w_a, w_b, w_c, w_o):
    B, L, _ = u.shape
    z = jnp.matmul(u, w_in_l)
    (fq, fk, fv, ff, sq, sk, sv, hq, hf, hi, hg, ga, gb, gc) = split_cols(z, IN_COLS)
    qa = fq.reshape(B, L, FOX_KV_HEADS, FOX_GROUP, HEAD_DIM)
    ka = fk.reshape(B, L, FOX_KV_HEADS, HEAD_DIM)
    va = fv.reshape(B, L, FOX_KV_HEADS, HEAD_DIM)
    logf_a = jax.nn.log_sigmoid(ff.astype(jnp.float32) + fgate_b.astype(jnp.float32))
    qb = sq.reshape(B, L, SB_KV_HEADS, SB_GROUP, HEAD_DIM)
    kb = sk.reshape(B, L, SB_KV_HEADS, HEAD_DIM)
    vb = sv.reshape(B, L, SB_KV_HEADS, HEAD_DIM)
    xf = hf.astype(jnp.float32)
    ls = jax.nn.log_sigmoid(xf)
    lb_pos = lb > 0
    log_lb = jnp.log(jnp.where(lb_pos, lb, 1.0))
    logf_c = jnp.where(lb_pos, jnp.logaddexp(log_lb, jnp.log1p(-lb) + ls), ls)
    k_c = (1.0 - lb) * jax.nn.sigmoid(-xf)
    q_c = hq.astype(jnp.float32) * HGRN_EXPAND ** -0.5
    v_c = hi.astype(jnp.float32)
    if past is None:
        offset = 0
        ka_all, va_all, logf_all, kb_all, vb_all = ka, va, logf_a, kb, vb
        s0 = jnp.zeros((B, HGRN_HEADS, HGRN_EXPAND, HGRN_VDIM), jnp.float32)
        chunk = HGRN_CHUNK
    else:
        pka, pva, plogf, pkb, pvb, s0 = past
        offset = pka.shape[1]
        ka_all = jnp.concatenate([pka.astype(ka.dtype), ka], axis=1)
        va_all = jnp.concatenate([pva.astype(va.dtype), va], axis=1)
        logf_all = jnp.concatenate([plogf.astype(jnp.float32), logf_a], axis=1)
        kb_all = jnp.concatenate([pkb.astype(kb.dtype), kb], axis=1)
        vb_all = jnp.concatenate([pvb.astype(vb.dtype), vb], axis=1)
        s0 = s0.astype(jnp.float32)
        chunk = L
    qpos = offset + jnp.arange(L)
    kpos = jnp.arange(offset + L)
    c_all = jnp.cumsum(logf_all, axis=1).reshape(B, offset + L, FOX_KV_HEADS, FOX_GROUP)
    o_a = sweep_queries(fox_block, (qa, c_all[:, offset:]), qpos, (ka_all, va_all, c_all, kpos))
    o_b = sweep_queries(sb_block, (qb,), qpos, (kb_all, vb_all, kpos))
    o_c, s_fin = hgrn_chunked(q_c.reshape(B, L, HGRN_HEADS, HGRN_EXPAND),
                              k_c.reshape(B, L, HGRN_HEADS, HGRN_EXPAND),
                              v_c.reshape(B, L, HGRN_HEADS, HGRN_VDIM),
                              logf_c.reshape(B, L, HGRN_HEADS, HGRN_EXPAND), s0, chunk)
    o_c = o_c * lax.rsqrt(jnp.mean(o_c * o_c, axis=-1, keepdims=True) + NORM_EPS)
    o_c = o_c * gnorm_g.astype(jnp.float32) * jax.nn.silu(hg.astype(jnp.float32).reshape(B, L, HGRN_HEADS, HGRN_VDIM))
    y_a = jnp.matmul(o_a.reshape(B, L, FOX_WIDTH), w_a)
    y_b = jnp.matmul(o_b.reshape(B, L, SB_WIDTH), w_b)
    y_c = jnp.matmul(o_c.reshape(B, L, HGRN_WIDTH).astype(u.dtype), w_c)
    merged = jax.nn.sigmoid(ga) * y_a + jax.nn.sigmoid(gb) * y_b + jax.nn.sigmoid(gc) * y_c
    return jnp.matmul(merged, w_o), (ka, va, logf_a, kb, vb, s_fin)


def moe_ffn(u, router_w, router_b, w1, b1, w2, b2):
    B, L, D = u.shape
    T = B * L
    xf = u.reshape(T, D)
    logits = jnp.matmul(xf, router_w).astype(jnp.float32) + router_b.astype(jnp.float32)
    top_v, top_i = lax.top_k(logits, TOP_K)
    gates = jax.nn.softmax(top_v, axis=-1)
    tk = T * TOP_K
    blk = max(8, min(MOE_BLOCK, tk // N_EXPERTS))
    n_blocks = -(-tk // blk) + N_EXPERTS
    flat_e = top_i.reshape(tk)
    order = jnp.argsort(flat_e)
    se = flat_e[order]
    st = (jnp.arange(tk) // TOP_K)[order]
    sg = gates.reshape(tk)[order]
    counts = jnp.bincount(flat_e, length=N_EXPERTS)
    padded = (counts + blk - 1) // blk * blk
    ends = jnp.cumsum(padded)
    dest = (ends - padded)[se] + jnp.arange(tk) - (jnp.cumsum(counts) - counts)[se]
    slot_tok = jnp.full((n_blocks * blk,), T, jnp.int32).at[dest].set(st)
    slot_gate = jnp.zeros((n_blocks * blk,), jnp.float32).at[dest].set(sg)
    block_exp = jnp.minimum(jnp.searchsorted(ends, jnp.arange(n_blocks) * blk, side='right'), N_EXPERTS - 1)
    x_pad = jnp.concatenate([xf, jnp.zeros((1, D), xf.dtype)], axis=0)

    def expert_block(args):
        tok, e = args
        h = (jnp.matmul(x_pad[tok], w1[e]) + b1[e]).astype(jnp.float32)
        glu = jnp.minimum(h[:, :D_EXPERT], SWIGLU_LIMIT)
        lin = jnp.clip(h[:, D_EXPERT:], -SWIGLU_LIMIT, SWIGLU_LIMIT)
        act = glu * jax.nn.sigmoid(SWIGLU_ALPHA * glu) * (lin + 1.0)
        return jnp.matmul(act.astype(xf.dtype), w2[e]) + b2[e]

    y_slots = lax.map(expert_block, (slot_tok.reshape(n_blocks, blk), block_exp))
    y_slots = y_slots.reshape(n_blocks * blk, D) * slot_gate[:, None].astype(xf.dtype)
    y = jax.ops.segment_sum(y_slots, slot_tok, num_segments=T + 1)[:T]
    return y.reshape(B, L, D)


def trunk_layer(x, c, past, ada_w, ada_b, g_mix, w_in_l, fgate_b, lb, gnorm_g, w_a, w_b, w_c, w_o,
                g_ffn, router_w, router_b, w1, b1, w2, b2):
    mod = jnp.matmul(jax.nn.silu(c), ada_w) + ada_b
    sh_a, sc_a, gt_a, sh_m, sc_m, gt_m = jnp.split(mod[:, None, :], N_MOD, axis=-1)
    u = rmsnorm(x, g_mix) * (1.0 + sc_a) + sh_a
    mix, entries = token_mixers(u, past, w_in_l, fgate_b, lb, gnorm_g, w_a, w_b, w_c, w_o)
    x = x + gt_a * mix
    u2 = rmsnorm(x, g_ffn) * (1.0 + sc_m) + sh_m
    x = x + gt_m * moe_ffn(u2, router_w, router_b, w1, b1, w2, b2)
    return x, entries


def setup_inputs(seed: int = 0) -> dict:
    key = jax.random.key(seed)
    k = jax.random.split(key, 32)
    D = D_MODEL

    def nrm(i, shape, scale):
        return jax.random.normal(k[i], shape, jnp.float32) * scale

    n_pages = PAST_LEN // PAGE_SIZE
    n_used = DEC_BATCH * n_pages
    n_pool = n_used + n_used // 4
    page_table = jax.random.permutation(k[0], n_pool)[:n_used].reshape(DEC_BATCH, n_pages).astype(jnp.int32)
    return {
        'x_prompt': nrm(1, (BATCH, SEQ, D), 1.0),
        'x_sample': nrm(2, (DEC_BATCH, DEC_SEQ, D), 1.0),
        'cache_fox_k': nrm(3, (DEPTH, n_pool, PAGE_SIZE, FOX_KV_HEADS, HEAD_DIM), 1.0),
        'cache_fox_v': nrm(4, (DEPTH, n_pool, PAGE_SIZE, FOX_KV_HEADS, HEAD_DIM), 1.0),
        'cache_fox_logf': jax.nn.log_sigmoid(FOX_FGATE_BIAS + nrm(5, (DEPTH, n_pool, PAGE_SIZE, FOX_HEADS), 1.0)),
        'cache_sb_k': nrm(6, (DEPTH, n_pool, PAGE_SIZE, SB_KV_HEADS, HEAD_DIM), 1.0),
        'cache_sb_v': nrm(7, (DEPTH, n_pool, PAGE_SIZE, SB_KV_HEADS, HEAD_DIM), 1.0),
        'state_hgrn': nrm(8, (DEPTH, DEC_BATCH, HGRN_HEADS, HGRN_EXPAND, HGRN_VDIM), 0.5),
        'page_table': page_table,
        'c_prompt': nrm(9, (BATCH, D), 1.0),
        'c_sample': nrm(10, (DEC_BATCH, D), 1.0),
        'ada_w': nrm(11, (DEPTH, D, N_MOD * D), 0.5 * D ** -0.5),
        'ada_b': nrm(12, (DEPTH, N_MOD * D), 0.01),
        'norm_mix_g': 1.0 + nrm(13, (DEPTH, D), 0.02),
        'w_in': nrm(14, (DEPTH, D, IN_WIDTH), D ** -0.5),
        'fox_fgate_b': FOX_FGATE_BIAS + nrm(15, (DEPTH, FOX_HEADS), 0.1),
        'hgrn_lb_logits': nrm(16, (DEPTH, HGRN_FDIM), 0.1),
        'hgrn_gnorm_g': 1.0 + nrm(17, (DEPTH, HGRN_VDIM), 0.02),
        'w_branch_a': nrm(18, (DEPTH, FOX_WIDTH, D), FOX_WIDTH ** -0.5),
        'w_branch_b': nrm(19, (DEPTH, SB_WIDTH, D), SB_WIDTH ** -0.5),
        'w_branch_c': nrm(20, (DEPTH, HGRN_WIDTH, D), HGRN_WIDTH ** -0.5),
        'w_out': nrm(21, (DEPTH, D, D), D ** -0.5),
        'norm_ffn_g': 1.0 + nrm(22, (DEPTH, D), 0.02),
        'router_w': nrm(23, (DEPTH, D, N_EXPERTS), D ** -0.5),
        'router_b': nrm(24, (DEPTH, N_EXPERTS), 0.01),
        'moe_w1': nrm(25, (DEPTH, N_EXPERTS, D, 2 * D_EXPERT), D ** -0.5),
        'moe_b1': nrm(26, (DEPTH, N_EXPERTS, 2 * D_EXPERT), 0.01),
        'moe_w2': nrm(27, (DEPTH, N_EXPERTS, D_EXPERT, D), D_EXPERT ** -0.5),
        'moe_b2': nrm(28, (DEPTH, N_EXPERTS, D), 0.01),
        'final_norm_g': 1.0 + nrm(29, (D,), 0.02),
    }


def reference(x_prompt, x_sample, cache_fox_k, cache_fox_v, cache_fox_logf, cache_sb_k, cache_sb_v,
              state_hgrn, page_table, c_prompt, c_sample, ada_w, ada_b, norm_mix_g, w_in, fox_fgate_b,
              hgrn_lb_logits, hgrn_gnorm_g, w_branch_a, w_branch_b, w_branch_c, w_out, norm_ffn_g,
              router_w, router_b, moe_w1, moe_b1, moe_w2, moe_b2, final_norm_g):
    lb_sm = jax.nn.softmax(hgrn_lb_logits.astype(jnp.float32), axis=0)
    hgrn_lb = jnp.cumsum(lb_sm, axis=0) - lb_sm[0]

    def run_group(x, c, sample):
        entries = []
        for l in range(DEPTH):
            past = None
            if sample:
                past = (gather_pages(cache_fox_k[l], page_table), gather_pages(cache_fox_v[l], page_table),
                        gather_pages(cache_fox_logf[l], page_table), gather_pages(cache_sb_k[l], page_table),
                        gather_pages(cache_sb_v[l], page_table), state_hgrn[l])
            x, e = trunk_layer(x, c, past, ada_w[l], ada_b[l], norm_mix_g[l], w_in[l], fox_fgate_b[l],
                               hgrn_lb[l], hgrn_gnorm_g[l], w_branch_a[l], w_branch_b[l], w_branch_c[l],
                               w_out[l], norm_ffn_g[l], router_w[l], router_b[l], moe_w1[l], moe_b1[l],
                               moe_w2[l], moe_b2[l])
            entries.append(e)
        return rmsnorm(x, final_norm_g), [jnp.stack(col) for col in zip(*entries)]

    y_prompt, (p_fox_k, p_fox_v, p_fox_logf, p_sb_k, p_sb_v, p_hgrn) = run_group(x_prompt, c_prompt, False)
    y_sample, (s_fox_k, s_fox_v, s_fox_logf, s_sb_k, s_sb_v, s_hgrn) = run_group(x_sample, c_sample, True)
    return (y_prompt, y_sample, p_fox_k, p_fox_v, p_fox_logf, p_sb_k, p_sb_v, p_hgrn,
            s_fox_k, s_fox_v, s_fox_logf, s_sb_k, s_sb_v, s_hgrn)
```

```python
import functools

import jax
import jax.numpy as jnp
from jax import lax
from jax.experimental import pallas as pl
from jax.experimental.pallas import tpu as pltpu

BF = jnp.bfloat16
F32 = jnp.float32

D_MODEL = 4096
BATCH = 4
SEQ = 2048
DEPTH = 4
DEC_BATCH = 8
DEC_SEQ = 4
PAST_LEN = 8192
PAGE_SIZE = 128
HEAD_DIM = 128
FOX_HEADS = 12
FOX_KV_HEADS = 4
SB_HEADS = 12
SB_KV_HEADS = 4
GROUP = 3
HGRN_HEADS = 8
HGRN_EXPAND = 128
HGRN_VDIM = 128
HGRN_CHUNK = 64
HGRN_SUB = 16
N_EXPERTS = 32
TOP_K = 4
D_EXPERT = D_MODEL // 4
SWIGLU_LIMIT = 7.0
SWIGLU_ALPHA = 1.702
NORM_EPS = 1e-6
NEG_BIG = -1e30
N_MOD = 6
ATT_SCALE = HEAD_DIM ** -0.5

FOX_WIDTH = FOX_HEADS * HEAD_DIM
FOX_KV_WIDTH = FOX_KV_HEADS * HEAD_DIM
SB_WIDTH = SB_HEADS * HEAD_DIM
SB_KV_WIDTH = SB_KV_HEADS * HEAD_DIM
HGRN_FDIM = HGRN_HEADS * HGRN_EXPAND
HGRN_WIDTH = HGRN_HEADS * HGRN_VDIM
IN_COLS = (FOX_WIDTH, FOX_KV_WIDTH, FOX_KV_WIDTH, FOX_HEADS,
           SB_WIDTH, SB_KV_WIDTH, SB_KV_WIDTH,
           HGRN_FDIM, HGRN_FDIM, HGRN_WIDTH, HGRN_WIDTH,
           D_MODEL, D_MODEL, D_MODEL)

LANES = 128
VMEM_LIMIT_BYTES = 56 * 1024 * 1024

ROW_TILE = 256
N_PROMPT = BATCH * SEQ
N_SAMPLE = DEC_BATCH * DEC_SEQ
N_TOK = N_PROMPT + N_SAMPLE
R_ROWS = N_PROMPT + ROW_TILE
MM_TM = 768
MM_TN = 512

C_GA = 0
C_GB = C_GA + D_MODEL
C_GC = C_GB + D_MODEL
C_FQ = C_GC + D_MODEL
C_SQ = C_FQ + FOX_WIDTH
C_FK = C_SQ + SB_WIDTH
C_FV = C_FK + FOX_KV_WIDTH
C_SK = C_FV + FOX_KV_WIDTH
C_SV = C_SK + SB_KV_WIDTH
C_HQ = C_SV + SB_KV_WIDTH
C_HF = C_HQ + HGRN_FDIM
C_HI = C_HF + HGRN_FDIM
C_HG = C_HI + HGRN_WIDTH
C_FF = C_HG + HGRN_WIDTH
C_END = C_FF + LANES
Z_WIDTH = -(-C_END // MM_TN) * MM_TN

MOE_BLOCK = 128
N_SLOT_TOK = N_TOK * TOP_K
N_MOE_BLOCKS = -(-N_SLOT_TOK // MOE_BLOCK) + N_EXPERTS
N_SLOTS = N_MOE_BLOCKS * MOE_BLOCK


def _params(sem):
    return pltpu.CompilerParams(dimension_semantics=sem, vmem_limit_bytes=VMEM_LIMIT_BYTES)


def _mm_kernel(*refs, has_bias, cast_w):
    x_ref, w_ref = refs[0], refs[1]
    pos = 2
    b_ref = None
    if has_bias:
        b_ref = refs[pos]
        pos += 1
    o_ref = refs[pos]
    if cast_w:
        wbf_ref = refs[pos + 1]

        @pl.when(pl.program_id(1) == 0)
        def _():
            wbf_ref[...] = w_ref[...].astype(BF)

        w = wbf_ref[...]
    else:
        w = w_ref[...]
    acc = jnp.dot(x_ref[...], w, preferred_element_type=F32)
    if has_bias:
        acc = acc + b_ref[...]
    o_ref[...] = acc.astype(o_ref.dtype)


def matmul(x, w, *, layer=None, bias=None, tm, tn, out_dtype):
    m, k = x.shape
    n = w.shape[-1]
    assert m % tm == 0 and n % tn == 0
    cast_w = w.dtype != BF
    if w.ndim == 3:
        w_spec = pl.BlockSpec((None, k, tn), lambda j, i: (layer, 0, j))
    else:
        w_spec = pl.BlockSpec((k, tn), lambda j, i: (0, j))
    in_specs = [pl.BlockSpec((tm, k), lambda j, i: (i, 0)), w_spec]
    args = [x, w]
    if bias is not None:
        in_specs.append(pl.BlockSpec((1, tn), lambda j, i: (0, j)))
        args.append(bias)
    scratch = [pltpu.VMEM((k, tn), BF)] if cast_w else []
    return pl.pallas_call(
        functools.partial(_mm_kernel, has_bias=bias is not None, cast_w=cast_w),
        grid=(n // tn, m // tm),
        in_specs=in_specs,
        out_specs=pl.BlockSpec((tm, tn), lambda j, i: (i, j)),
        out_shape=jax.ShapeDtypeStruct((m, n), out_dtype),
        scratch_shapes=scratch,
        compiler_params=_params(("arbitrary", "arbitrary")),
        name="dense_matmul",
    )(*args)


def _ada_kernel(c_ref, w_ref, b_ref, o_ref):
    c = c_ref[...]
    s = (c * jax.nn.sigmoid(c)).astype(BF)
    acc = jnp.dot(s, w_ref[...].astype(BF), preferred_element_type=F32)
    o_ref[...] = acc + b_ref[...]


def adaln_mod(c_all, ada_w, ada_b, *, tn=1024):
    depth, d, n = ada_w.shape
    rows = c_all.shape[0]
    return pl.pallas_call(
        _ada_kernel,
        grid=(depth, n // tn),
        in_specs=[pl.BlockSpec((rows, d), lambda l, j: (0, 0)),
                  pl.BlockSpec((None, d, tn), lambda l, j: (l, 0, j)),
                  pl.BlockSpec((None, 1, tn), lambda l, j: (l, 0, j))],
        out_specs=pl.BlockSpec((None, rows, tn), lambda l, j: (l, 0, j)),
        out_shape=jax.ShapeDtypeStruct((depth, rows, n), F32),
        compiler_params=_params(("arbitrary", "arbitrary")),
        name="adaln_mod",
    )(c_all, ada_w, ada_b.reshape(depth, 1, n))


def _norm_kernel(*refs, has_delta, modulate, write_x, n_prompt_blocks, n_sample):
    it = iter(refs)
    x_ref = next(it)
    if has_delta:
        d_ref, gtp_ref, gts_ref = next(it), next(it), next(it)
    g_ref = next(it)
    if modulate:
        scp_ref, shp_ref, scs_ref, shs_ref = next(it), next(it), next(it), next(it)
    if write_x:
        xo_ref = next(it)
    u_ref = next(it)
    i = pl.program_id(0)

    def body(x, d, gt, sc, sh):
        if has_delta:
            x = x + gt * d
        ms = jnp.mean(x * x, axis=-1, keepdims=True)
        y = x * lax.rsqrt(ms + NORM_EPS) * g_ref[...]
        if modulate:
            y = y * (1.0 + sc) + sh
        return x, y

    @pl.when(i < n_prompt_blocks)
    def _():
        x, y = body(x_ref[...],
                    d_ref[...] if has_delta else None,
                    gtp_ref[...] if has_delta else None,
                    scp_ref[...] if modulate else None,
                    shp_ref[...] if modulate else None)
        if write_x:
            xo_ref[...] = x
        u_ref[...] = y.astype(u_ref.dtype)

    @pl.when(i >= n_prompt_blocks)
    def _():
        rows = pl.ds(0, n_sample)
        x, y = body(x_ref[rows, :],
                    d_ref[rows, :] if has_delta else None,
                    gts_ref[...] if has_delta else None,
                    scs_ref[...] if modulate else None,
                    shs_ref[...] if modulate else None)
        if write_x:
            xo_ref[...] = jnp.zeros_like(xo_ref)
            xo_ref[rows, :] = x
        u_ref[...] = jnp.zeros_like(u_ref)
        u_ref[rows, :] = y.astype(u_ref.dtype)


def norm_rows(x, g, *, layer, mod_p=None, mod_s=None, k_scale=None, k_shift=None,
              delta=None, k_gate=None, gate_layer=None, write_x=False, out_dtype=BF, rows_per_batch=SEQ,
              n_prompt=N_PROMPT, n_sample=N_SAMPLE, tile=ROW_TILE):
    if gate_layer is None:
        gate_layer = layer
    r, d = x.shape
    npb = n_prompt // tile
    bpb = rows_per_batch // tile
    has_delta = delta is not None
    modulate = k_scale is not None
    row_spec = pl.BlockSpec((tile, d), lambda i: (i, 0))

    def pspec(k, lyr=layer):
        return pl.BlockSpec((None, None, 1, d), lambda i: (lyr, jnp.minimum(i, npb - 1) // bpb, 0, k))

    def sspec(k, lyr=layer):
        return pl.BlockSpec((None, n_sample, d), lambda i: (lyr, 0, k))

    args, specs = [x], [row_spec]
    if has_delta:
        args += [delta, mod_p, mod_s]
        specs += [row_spec, pspec(k_gate, gate_layer), sspec(k_gate, gate_layer)]
    if g.ndim == 2:
        args.append(g.reshape(g.shape[0], 1, d))
        specs.append(pl.BlockSpec((None, 1, d), lambda i: (layer, 0, 0)))
    else:
        args.append(g.reshape(1, d))
        specs.append(pl.BlockSpec((1, d), lambda i: (0, 0)))
    if modulate:
        args += [mod_p, mod_p, mod_s, mod_s]
        specs += [pspec(k_scale), pspec(k_shift), sspec(k_scale), sspec(k_shift)]
    out_shape, out_specs = [], []
    if write_x:
        out_shape.append(jax.ShapeDtypeStruct((r, d), F32))
        out_specs.append(row_spec)
    out_shape.append(jax.ShapeDtypeStruct((r, d), out_dtype))
    out_specs.append(row_spec)
    res = pl.pallas_call(
        functools.partial(_norm_kernel, has_delta=has_delta, modulate=modulate, write_x=write_x,
                          n_prompt_blocks=npb, n_sample=n_sample),
        grid=(r // tile,),
        in_specs=specs,
        out_specs=out_specs,
        out_shape=out_shape,
        compiler_params=_params(("arbitrary",)),
        name="resid_norm",
    )(*args)
    return res if write_x else res[0]


def _log_sigmoid(x):
    return jnp.minimum(x, 0.0) - jnp.log1p(jnp.exp(-jnp.abs(x)))


def _split_hi_lo(x):
    hi = x.astype(BF)
    lo = (x - hi.astype(F32)).astype(BF)
    return hi, lo


def _fox_prompt_kernel(q_ref, k_ref, v_ref, cq_ref, ck_ref, o_ref, *, tq, group):
    qi = pl.program_id(2)
    row = qi * tq + lax.broadcasted_iota(jnp.int32, (tq, tq), 0)
    col0 = lax.broadcasted_iota(jnp.int32, (tq, tq), 1)
    for g in range(group):
        q = q_ref[:, g * HEAD_DIM:(g + 1) * HEAD_DIM]
        cq = cq_ref[g]

        def step(j, carry):
            m, l, acc = carry
            ks = pl.multiple_of(j * tq, tq)
            k = k_ref[pl.ds(ks, tq), :]
            v = v_ref[pl.ds(ks, tq), :]
            s = lax.dot_general(q, k, (((1,), (1,)), ((), ())), preferred_element_type=F32)
            ck = ck_ref[g, j]
            s = s * ATT_SCALE + (cq - ck)
            s = jnp.where(col0 + ks <= row, s, NEG_BIG)
            m_new = jnp.maximum(m, jnp.max(s, axis=-1, keepdims=True))
            a = jnp.exp(m - m_new)
            p = jnp.exp(s - m_new)
            l = a * l + jnp.sum(p, axis=-1, keepdims=True)
            acc = a * acc + jnp.dot(p.astype(BF), v, preferred_element_type=F32)
            return m_new, l, acc

        init = (jnp.full((tq, 1), NEG_BIG, F32), jnp.zeros((tq, 1), F32), jnp.zeros((tq, HEAD_DIM), F32))
        m, l, acc = lax.fori_loop(0, qi + 1, step, init)
        o_ref[:, g * HEAD_DIM:(g + 1) * HEAD_DIM] = (acc / l).astype(o_ref.dtype)


def _sb_prompt_kernel(q_ref, k_ref, v_ref, o_ref, *, tq, group):
    qi = pl.program_id(2)
    row = qi * tq + lax.broadcasted_iota(jnp.int32, (tq, tq), 0)
    col0 = lax.broadcasted_iota(jnp.int32, (tq, tq), 1)
    upper = (lax.broadcasted_iota(jnp.int32, (tq, tq), 0) > col0).astype(BF)
    for g in range(group):
        q = q_ref[:, g * HEAD_DIM:(g + 1) * HEAD_DIM]

        def step(jj, carry):
            later, acc = carry
            j = qi - jj
            ks = pl.multiple_of(j * tq, tq)
            k = k_ref[pl.ds(ks, tq), :]
            v = v_ref[pl.ds(ks, tq), :]
            z = lax.dot_general(q, k, (((1,), (1,)), ((), ())), preferred_element_type=F32) * ATT_SCALE
            mask = col0 + ks < row
            ln = jnp.where(mask, _log_sigmoid(-z), 0.0)
            hi, lo = _split_hi_lo(ln)
            after = (jnp.dot(hi, upper, preferred_element_type=F32)
                     + jnp.dot(lo, upper, preferred_element_type=F32)) + later
            a = jnp.where(mask, jnp.exp(_log_sigmoid(z) + after), 0.0)
            acc = acc + jnp.dot(a.astype(BF), v, preferred_element_type=F32)
            later = later + jnp.sum(ln, axis=-1, keepdims=True)
            return later, acc

        init = (jnp.zeros((tq, 1), F32), jnp.zeros((tq, HEAD_DIM), F32))
        _, acc = lax.fori_loop(0, qi + 1, step, init)
        o_ref[:, g * HEAD_DIM:(g + 1) * HEAD_DIM] = acc.astype(o_ref.dtype)


def prompt_attention(z, *, kind, c_q, c_k=None, batch=BATCH, seq=SEQ, kv_heads=FOX_KV_HEADS,
                     group=GROUP, col_q, col_k, col_v, tq=256):
    qw = group * HEAD_DIM
    nq = seq // tq
    assert col_q % qw == 0 and col_k % HEAD_DIM == 0 and col_v % HEAD_DIM == 0
    in_specs = [
        pl.BlockSpec((tq, qw), lambda b, h, qi: (b * nq + qi, col_q // qw + h)),
        pl.BlockSpec((seq, HEAD_DIM), lambda b, h, qi: (b, col_k // HEAD_DIM + h)),
        pl.BlockSpec((seq, HEAD_DIM), lambda b, h, qi: (b, col_v // HEAD_DIM + h)),
    ]
    args = [z, z, z]
    if kind == "fox":
        in_specs += [pl.BlockSpec((None, group, tq, 1), lambda b, h, qi: (b, h, qi, 0)),
                     pl.BlockSpec((None, group, nq, 1, tq), lambda b, h, qi: (b, h, 0, 0, 0))]
        args += [c_q, c_k]
        kern = functools.partial(_fox_prompt_kernel, tq=tq, group=group)
    else:
        kern = functools.partial(_sb_prompt_kernel, tq=tq, group=group)
    return pl.pallas_call(
        kern,
        grid=(batch, kv_heads, nq),
        in_specs=in_specs,
        out_specs=pl.BlockSpec((tq, qw), lambda b, h, qi: (b * nq + qi, h)),
        out_shape=jax.ShapeDtypeStruct((batch * seq, kv_heads * qw), BF),
        compiler_params=_params(("arbitrary", "arbitrary", "arbitrary")),
        name=kind + "_prompt_attention",
    )(*args)


Q_ROWS_PER_TOKEN = 16
N_QROWS = DEC_SEQ * Q_ROWS_PER_TOKEN
PAGE_COLS = PAGE_SIZE * FOX_KV_HEADS
N_PAGES = PAST_LEN // PAGE_SIZE


def _sample_masks(new_step, strict):
    r = lax.broadcasted_iota(jnp.int32, (N_QROWS, PAGE_COLS), 0)
    c = lax.broadcasted_iota(jnp.int32, (N_QROWS, PAGE_COLS), 1)
    head = r % Q_ROWS_PER_TOKEN
    t = r // Q_ROWS_PER_TOKEN
    pos = c // FOX_KV_HEADS
    ok = (head < FOX_HEADS) & ((c % FOX_KV_HEADS) == head // GROUP)
    causal_new = (pos < t) if strict else (pos <= t)
    return ok & (jnp.logical_not(new_step) | causal_new)


def _fox_sample_kernel(pt_ref, q_ref, kc_ref, vc_ref, kn_ref, vn_ref, cq_ref, ck_ref, o_ref,
                       m_ref, l_ref, acc_ref, *, n_pages):
    p = pl.program_id(1)
    new_step = p == n_pages

    @pl.when(p == 0)
    def _():
        m_ref[...] = jnp.full_like(m_ref, NEG_BIG)
        l_ref[...] = jnp.zeros_like(l_ref)
        acc_ref[...] = jnp.zeros_like(acc_ref)

    k = jnp.where(new_step, kn_ref[...], kc_ref[...]).astype(BF)
    v = jnp.where(new_step, vn_ref[...], vc_ref[...]).astype(BF)
    s = lax.dot_general(q_ref[...], k, (((1,), (1,)), ((), ())), preferred_element_type=F32)
    ck = ck_ref[...]
    bias = cq_ref[...] - jnp.concatenate([ck] * DEC_SEQ, axis=0)
    s = jnp.where(_sample_masks(new_step, False), s * ATT_SCALE + bias, NEG_BIG)
    m = m_ref[...]
    m_new = jnp.maximum(m, jnp.max(s, axis=-1, keepdims=True))
    a = jnp.exp(m - m_new)
    pr = jnp.where(s > 0.5 * NEG_BIG, jnp.exp(s - m_new), 0.0)
    l_ref[...] = a * l_ref[...] + jnp.sum(pr, axis=-1, keepdims=True)
    acc_ref[...] = a * acc_ref[...] + jnp.dot(pr.astype(BF), v, preferred_element_type=F32)
    m_ref[...] = m_new

    @pl.when(new_step)
    def _():
        l = l_ref[...]
        o_ref[...] = acc_ref[...] / jnp.where(l > 0.0, l, 1.0)


def _sb_sample_kernel(pt_ref, q_ref, kc_ref, vc_ref, kn_ref, vn_ref, o_ref, later_ref, acc_ref):
    p = pl.program_id(1)
    new_step = p == 0

    @pl.when(p == 0)
    def _():
        later_ref[...] = jnp.zeros_like(later_ref)
        acc_ref[...] = jnp.zeros_like(acc_ref)

    k = jnp.where(new_step, kn_ref[...], kc_ref[...]).astype(BF)
    v = jnp.where(new_step, vn_ref[...], vc_ref[...]).astype(BF)
    z = lax.dot_general(q_ref[...], k, (((1,), (1,)), ((), ())), preferred_element_type=F32) * ATT_SCALE
    mask = _sample_masks(new_step, True)
    ln = jnp.where(mask, _log_sigmoid(-z), 0.0)
    ci = lax.broadcasted_iota(jnp.int32, (PAGE_COLS, PAGE_COLS), 0) // FOX_KV_HEADS
    cj = lax.broadcasted_iota(jnp.int32, (PAGE_COLS, PAGE_COLS), 1) // FOX_KV_HEADS
    upper = (ci > cj).astype(BF)
    hi, lo = _split_hi_lo(ln)
    after = (jnp.dot(hi, upper, preferred_element_type=F32)
             + jnp.dot(lo, upper, preferred_element_type=F32)) + later_ref[...]
    a = jnp.where(mask, jnp.exp(_log_sigmoid(z) + after), 0.0)
    acc_ref[...] += jnp.dot(a.astype(BF), v, preferred_element_type=F32)
    later_ref[...] += jnp.sum(ln, axis=-1, keepdims=True)

    @pl.when(p == pl.num_programs(1) - 1)
    def _():
        o_ref[...] = acc_ref[...]


def sample_attention(q_rows, cache_k, cache_v, k_new, v_new, page_table, *, kind, layer,
                     c_q=None, c_k=None):
    db = q_rows.shape[0]
    n_pages = page_table.shape[1]
    if kind == "fox":
        def page_of(p, pt, b):
            return pt[b, jnp.minimum(p, n_pages - 1)]
    else:
        def page_of(p, pt, b):
            return pt[b, n_pages - jnp.maximum(p, 1)]
    q_spec = pl.BlockSpec((None, N_QROWS, HEAD_DIM), lambda b, p, pt: (b, 0, 0))
    c_spec = pl.BlockSpec((None, None, PAGE_COLS, HEAD_DIM), lambda b, p, pt: (layer, page_of(p, pt, b), 0, 0))
    n_spec = pl.BlockSpec((None, PAGE_COLS, HEAD_DIM), lambda b, p, pt: (b, 0, 0))
    o_spec = pl.BlockSpec((None, N_QROWS, HEAD_DIM), lambda b, p, pt: (b, 0, 0))
    in_specs = [q_spec, c_spec, c_spec, n_spec, n_spec]
    args = [q_rows, cache_k, cache_v, k_new, v_new]
    if kind == "fox":
        in_specs += [pl.BlockSpec((None, N_QROWS, 1), lambda b, p, pt: (b, 0, 0)),
                     pl.BlockSpec((None, None, Q_ROWS_PER_TOKEN, PAGE_COLS), lambda b, p, pt: (b, p, 0, 0))]
        args += [c_q, c_k]
        kern = functools.partial(_fox_sample_kernel, n_pages=n_pages)
        scratch = [pltpu.VMEM((N_QROWS, 1), F32), pltpu.VMEM((N_QROWS, 1), F32), pltpu.VMEM((N_QROWS, HEAD_DIM), F32)]
    else:
        kern = _sb_sample_kernel
        scratch = [pltpu.VMEM((N_QROWS, 1), F32), pltpu.VMEM((N_QROWS, HEAD_DIM), F32)]
    return pl.pallas_call(
        kern,
        grid_spec=pltpu.PrefetchScalarGridSpec(
            num_scalar_prefetch=1, grid=(db, n_pages + 1), in_specs=in_specs, out_specs=o_spec,
            scratch_shapes=scratch),
        out_shape=jax.ShapeDtypeStruct((db, N_QROWS, HEAD_DIM), F32),
        compiler_params=_params(("arbitrary", "arbitrary")),
        name=kind + "_sample_attention",
    )(page_table, *args)


def _hgrn_kernel(q_ref, f_ref, i_ref, g_ref, lb_ref, gn_ref, s0_ref, o_ref, sfin_ref, st_ref,
                 *, n_chunks, valid_len):
    c, sub = HGRN_CHUNK, HGRN_SUB
    ns = c // sub
    lb = lb_ref[...]
    lb_pos = lb > 0.0
    log_lb = jnp.log(jnp.where(lb_pos, lb, 1.0))
    log_1m = jnp.log1p(-lb)
    tri = (lax.broadcasted_iota(jnp.int32, (c, c), 0) >= lax.broadcasted_iota(jnp.int32, (c, c), 1)).astype(BF)
    blk_col = lax.broadcasted_iota(jnp.int32, (sub, c), 1)
    sub_row = lax.broadcasted_iota(jnp.int32, (sub, 1), 0)
    st_ref[...] = s0_ref[...]

    def chunk(ci, _):
        r0 = pl.multiple_of(ci * c, c)
        x = f_ref[pl.ds(r0, c), :].astype(F32)
        ls = _log_sigmoid(x)
        a = log_1m + ls
        mx = jnp.maximum(log_lb, a)
        lae = mx + jnp.log1p(jnp.exp(-jnp.abs(log_lb - a)))
        logf = jnp.where(lb_pos, lae, ls)
        kk = (1.0 - lb) * jax.nn.sigmoid(-x)
        if valid_len is not None:
            live = (ci * c + lax.broadcasted_iota(jnp.int32, (c, 1), 0)) < valid_len
            logf = jnp.where(live, logf, 0.0)
            kk = jnp.where(live, kk, 0.0)
        qq = q_ref[pl.ds(r0, c), :].astype(F32) * (HGRN_EXPAND ** -0.5)
        vv = i_ref[pl.ds(r0, c), :]
        hi, lo = _split_hi_lo(logf)
        gcum = jnp.dot(tri, hi, preferred_element_type=F32) + jnp.dot(tri, lo, preferred_element_type=F32)
        g_end = gcum[c - 1:c, :]
        st = st_ref[...]
        o = lax.dot_general((qq * jnp.exp(gcum)).astype(BF), st.astype(BF), (((1,), (1,)), ((), ())),
                            preferred_element_type=F32)
        a_rows = []
        for i in range(ns):
            gi = gcum[i * sub:(i + 1) * sub, :]
            qi_ = qq[i * sub:(i + 1) * sub, :]
            ki_ = kk[i * sub:(i + 1) * sub, :]
            if i > 0:
                ref = gcum[i * sub - 1:i * sub, :]
                qt = (qi_ * jnp.exp(gi - ref)).astype(BF)
                kt = (kk * jnp.exp(jnp.minimum(ref - gcum, 0.0))).astype(BF)
                off = lax.dot_general(qt, kt, (((1,), (1,)), ((), ())), preferred_element_type=F32)
                a_i = jnp.where(blk_col < i * sub, off, 0.0)
            else:
                a_i = jnp.zeros((sub, c), F32)
            for s in range(sub):
                dec = jnp.exp(jnp.where(sub_row >= s, gi - gi[s:s + 1, :], 0.0))
                colv = jnp.sum(qi_ * ki_[s:s + 1, :] * dec, axis=-1, keepdims=True)
                a_i = jnp.where((blk_col == i * sub + s) & (sub_row >= s), colv, a_i)
            a_rows.append(a_i)
        amat = jnp.concatenate(a_rows, axis=0)
        o = o + jnp.dot(amat.astype(BF), vv, preferred_element_type=F32)
        kd = (kk * jnp.exp(g_end - gcum)).astype(BF)
        vt = jnp.transpose(vv.astype(F32)).astype(BF)
        st_ref[...] = st * jnp.exp(g_end) + jnp.dot(vt, kd, preferred_element_type=F32)
        o = o * lax.rsqrt(jnp.mean(o * o, axis=-1, keepdims=True) + NORM_EPS)
        hg = g_ref[pl.ds(r0, c), :].astype(F32)
        o = o * gn_ref[...] * (hg * jax.nn.sigmoid(hg))
        o_ref[pl.ds(r0, c), :] = o.astype(o_ref.dtype)
        return 0

    lax.fori_loop(0, n_chunks, chunk, 0)
    sfin_ref[...] = st_ref[...]


def hgrn_mixer(z, lb, gnorm, s0_t, *, layer, batch, seq, col_q, col_f, col_i, col_g, valid_len=None,
               heads=HGRN_HEADS):
    n = HGRN_EXPAND

    def col(c0):
        return pl.BlockSpec((seq, n), lambda b, h: (b, c0 // n + h))

    return pl.pallas_call(
        functools.partial(_hgrn_kernel, n_chunks=seq // HGRN_CHUNK, valid_len=valid_len),
        grid=(batch, heads),
        in_specs=[col(col_q), col(col_f), col(col_i), col(col_g),
                  pl.BlockSpec((None, 1, n), lambda b, h: (layer, 0, h)),
                  pl.BlockSpec((None, 1, n), lambda b, h: (layer, 0, 0)),
                  pl.BlockSpec((None, None, n, n), lambda b, h: (b, h, 0, 0))],
        out_specs=[pl.BlockSpec((seq, n), lambda b, h: (b, h)),
                   pl.BlockSpec((None, None, n, n), lambda b, h: (b, h, 0, 0))],
        out_shape=[jax.ShapeDtypeStruct((batch * seq, heads * n), BF),
                   jax.ShapeDtypeStruct((batch, heads, n, n), F32)],
        scratch_shapes=[pltpu.VMEM((n, n), F32)],
        compiler_params=_params(("arbitrary", "arbitrary")),
        name="hgrn_mixer",
    )(z, z, z, z, lb.reshape(lb.shape[0], 1, -1), gnorm.reshape(gnorm.shape[0], 1, -1), s0_t)


def _merge_kernel(oa_ref, ob_ref, oc_ref, wa_ref, wb_ref, wc_ref, ga_ref, gb_ref, gc_ref, o_ref,
                  wa_bf, wb_bf, wc_bf):
    @pl.when(pl.program_id(1) == 0)
    def _():
        wa_bf[...] = wa_ref[...].astype(BF)
        wb_bf[...] = wb_ref[...].astype(BF)
        wc_bf[...] = wc_ref[...].astype(BF)

    ya = jnp.dot(oa_ref[...], wa_bf[...], preferred_element_type=F32)
    yb = jnp.dot(ob_ref[...], wb_bf[...], preferred_element_type=F32)
    yc = jnp.dot(oc_ref[...], wc_bf[...], preferred_element_type=F32)
    y = (jax.nn.sigmoid(ga_ref[...].astype(F32)) * ya + jax.nn.sigmoid(gb_ref[...].astype(F32)) * yb
         + jax.nn.sigmoid(gc_ref[...].astype(F32)) * yc)
    o_ref[...] = y.astype(o_ref.dtype)


def merge_branches(o_a, o_b, o_c, w_a, w_b, w_c, z, *, layer, tm=MM_TM, tn=MM_TN):
    r = o_a.shape[0]
    d = w_a.shape[-1]

    def lhs(a):
        return pl.BlockSpec((tm, a.shape[1]), lambda j, i: (i, 0))

    def wsp(w):
        return pl.BlockSpec((None, w.shape[1], tn), lambda j, i: (layer, 0, j))

    def gate(c0):
        return pl.BlockSpec((tm, tn), lambda j, i: (i, c0 // tn + j))

    return pl.pallas_call(
        _merge_kernel,
        grid=(d // tn, r // tm),
        in_specs=[lhs(o_a), lhs(o_b), lhs(o_c), wsp(w_a), wsp(w_b), wsp(w_c),
                  gate(C_GA), gate(C_GB), gate(C_GC)],
        out_specs=pl.BlockSpec((tm, tn), lambda j, i: (i, j)),
        out_shape=jax.ShapeDtypeStruct((r, d), BF),
        scratch_shapes=[pltpu.VMEM((w_a.shape[1], tn), BF), pltpu.VMEM((w_b.shape[1], tn), BF),
                        pltpu.VMEM((w_c.shape[1], tn), BF)],
        compiler_params=_params(("arbitrary", "arbitrary")),
        name="merge_branches",
    )(o_a, o_b, o_c, w_a, w_b, w_c, z, z, z)


def _router_kernel(x_ref, w_ref, b_ref, o_ref):
    xh, xl = _split_hi_lo(x_ref[...])
    wh, wl = _split_hi_lo(w_ref[...])
    acc = jnp.dot(xh, wh, preferred_element_type=F32)
    acc = acc + jnp.dot(xh, wl, preferred_element_type=F32)
    acc = acc + jnp.dot(xl, wh, preferred_element_type=F32)
    o_ref[...] = acc + b_ref[...]


def router_logits(u, router_w, router_b, *, layer, tile=ROW_TILE):
    r, d = u.shape
    return pl.pallas_call(
        _router_kernel,
        grid=(r // tile,),
        in_specs=[pl.BlockSpec((tile, d), lambda i: (i, 0)),
                  pl.BlockSpec((None, d, LANES), lambda i: (layer, 0, 0)),
                  pl.BlockSpec((None, 1, LANES), lambda i: (layer, 0, 0))],
        out_specs=pl.BlockSpec((tile, LANES), lambda i: (i, 0)),
        out_shape=jax.ShapeDtypeStruct((r, LANES), F32),
        compiler_params=_params(("arbitrary",)),
        name="router_logits",
    )(u, router_w, router_b)


def _gather_kernel(tok_ref, u_hbm, o_ref, buf, sem, *, block):
    def row_copy(r):
        return pltpu.make_async_copy(u_hbm.at[pl.ds(tok_ref[0, r], 1)], buf.at[pl.ds(r, 1)], sem)

    def start(r, _):
        row_copy(r).start()
        return 0

    def wait(r, _):
        row_copy(r).wait()
        return 0

    lax.fori_loop(0, block, start, 0)
    lax.fori_loop(0, block, wait, 0)
    o_ref[...] = buf[...].astype(o_ref.dtype)


def gather_rows(u, slot_tok, *, block=MOE_BLOCK):
    d = u.shape[1]
    n_slots = slot_tok.shape[0]
    return pl.pallas_call(
        functools.partial(_gather_kernel, block=block),
        grid=(n_slots // block,),
        in_specs=[pl.BlockSpec((None, 1, block), lambda i: (i, 0, 0), memory_space=pltpu.SMEM),
                  pl.BlockSpec(memory_space=pl.ANY)],
        out_specs=pl.BlockSpec((block, d), lambda i: (i, 0)),
        scratch_shapes=[pltpu.VMEM((block, d), F32), pltpu.SemaphoreType.DMA(())],
        out_shape=jax.ShapeDtypeStruct((n_slots, d), BF),
        compiler_params=_params(("arbitrary",)),
        name="moe_gather_rows",
    )(slot_tok.reshape(n_slots // block, 1, block), u)


def _expert_changed(be_ref, b):
    prev = be_ref[jnp.maximum(b - 1, 0)]
    return jnp.logical_or(b == 0, be_ref[b] != prev)


def _moe1_kernel(be_ref, nu_ref, x_ref, wg_ref, wl_ref, bg_ref, bl_ref, o_ref, wg_bf, wl_bf):
    b = pl.program_id(1)

    @pl.when(_expert_changed(be_ref, b))
    def _():
        wg_bf[...] = wg_ref[...].astype(BF)
        wl_bf[...] = wl_ref[...].astype(BF)

    @pl.when(b < nu_ref[0])
    def _():
        x = x_ref[...]
        hg = jnp.dot(x, wg_bf[...], preferred_element_type=F32) + bg_ref[...]
        hl = jnp.dot(x, wl_bf[...], preferred_element_type=F32) + bl_ref[...]
        glu = jnp.minimum(hg, SWIGLU_LIMIT)
        lin = jnp.clip(hl, -SWIGLU_LIMIT, SWIGLU_LIMIT)
        act = glu * jax.nn.sigmoid(SWIGLU_ALPHA * glu) * (lin + 1.0)
        o_ref[...] = act.astype(o_ref.dtype)

    @pl.when(b >= nu_ref[0])
    def _():
        o_ref[...] = jnp.zeros_like(o_ref)


def moe_up(x_sorted, w1, b1, block_exp, n_used, *, layer, block=MOE_BLOCK, tn=512):
    n_slots, d = x_sorted.shape
    de = w1.shape[-1] // 2
    nb = n_slots // block
    nj = de // tn
    b1r = b1.reshape(b1.shape[0], b1.shape[1], 1, b1.shape[2])
    return pl.pallas_call(
        _moe1_kernel,
        grid_spec=pltpu.PrefetchScalarGridSpec(
            num_scalar_prefetch=2, grid=(nj, nb),
            in_specs=[pl.BlockSpec((block, d), lambda j, b, be, nu: (b, 0)),
                      pl.BlockSpec((None, None, d, tn), lambda j, b, be, nu: (layer, be[b], 0, j)),
                      pl.BlockSpec((None, None, d, tn), lambda j, b, be, nu: (layer, be[b], 0, nj + j)),
                      pl.BlockSpec((None, None, 1, tn), lambda j, b, be, nu: (layer, be[b], 0, j)),
                      pl.BlockSpec((None, None, 1, tn), lambda j, b, be, nu: (layer, be[b], 0, nj + j))],
            out_specs=pl.BlockSpec((block, tn), lambda j, b, be, nu: (b, j)),
            scratch_shapes=[pltpu.VMEM((d, tn), BF), pltpu.VMEM((d, tn), BF)]),
        out_shape=jax.ShapeDtypeStruct((n_slots, de), BF),
        compiler_params=_params(("arbitrary", "arbitrary")),
        name="moe_up",
    )(block_exp, n_used, x_sorted, w1, w1, b1r, b1r)


def _moe2_kernel(be_ref, nu_ref, a_ref, w_ref, b_ref, gate_ref, o_ref, w_bf):
    b = pl.program_id(1)

    @pl.when(_expert_changed(be_ref, b))
    def _():
        w_bf[...] = w_ref[...].astype(BF)

    @pl.when(b < nu_ref[0])
    def _():
        y = jnp.dot(a_ref[...], w_bf[...], preferred_element_type=F32) + b_ref[...]
        o_ref[...] = y * gate_ref[...]

    @pl.when(b >= nu_ref[0])
    def _():
        o_ref[...] = jnp.zeros_like(o_ref)


def moe_down(act, w2, b2, slot_gate, block_exp, n_used, *, layer, block=MOE_BLOCK, tn=1024):
    n_slots, de = act.shape
    d = w2.shape[-1]
    b2r = b2.reshape(b2.shape[0], b2.shape[1], 1, d)
    return pl.pallas_call(
        _moe2_kernel,
        grid_spec=pltpu.PrefetchScalarGridSpec(
            num_scalar_prefetch=2, grid=(d // tn, n_slots // block),
            in_specs=[pl.BlockSpec((block, de), lambda j, b, be, nu: (b, 0)),
                      pl.BlockSpec((None, None, de, tn), lambda j, b, be, nu: (layer, be[b], 0, j)),
                      pl.BlockSpec((None, None, 1, tn), lambda j, b, be, nu: (layer, be[b], 0, j)),
                      pl.BlockSpec((block, 1), lambda j, b, be, nu: (b, 0))],
            out_specs=pl.BlockSpec((block, tn), lambda j, b, be, nu: (b, j)),
            scratch_shapes=[pltpu.VMEM((de, tn), BF)]),
        out_shape=jax.ShapeDtypeStruct((n_slots, d), F32),
        compiler_params=_params(("arbitrary", "arbitrary")),
        name="moe_down",
    )(block_exp, n_used, act, w2, b2r, slot_gate)


def _combine_kernel(dest_ref, y_hbm, o_ref, buf, sem, *, block, top_k):
    def row_copy(n):
        r = n // top_k
        k = n % top_k
        return pltpu.make_async_copy(y_hbm.at[pl.ds(dest_ref[0, n], 1)], buf.at[k, pl.ds(r, 1)], sem)

    def start(n, _):
        row_copy(n).start()
        return 0

    def wait(n, _):
        row_copy(n).wait()
        return 0

    lax.fori_loop(0, block * top_k, start, 0)
    lax.fori_loop(0, block * top_k, wait, 0)
    acc = buf[0]
    for k in range(1, top_k):
        acc = acc + buf[k]
    o_ref[...] = acc


def combine_rows(y_slots, dest, *, block=128, top_k=TOP_K):
    d = y_slots.shape[1]
    r = dest.shape[0] // top_k
    return pl.pallas_call(
        functools.partial(_combine_kernel, block=block, top_k=top_k),
        grid=(r // block,),
        in_specs=[pl.BlockSpec((None, 1, block * top_k), lambda i: (i, 0, 0), memory_space=pltpu.SMEM),
                  pl.BlockSpec(memory_space=pl.ANY)],
        out_specs=pl.BlockSpec((block, d), lambda i: (i, 0)),
        scratch_shapes=[pltpu.VMEM((top_k, block, d), F32), pltpu.SemaphoreType.DMA(())],
        out_shape=jax.ShapeDtypeStruct((r, d), F32),
        compiler_params=_params(("arbitrary",)),
        name="moe_combine_rows",
    )(dest.reshape(r // block, 1, block * top_k), y_slots)


def moe_dispatch(logits, *, n_tok=N_TOK, n_rows=R_ROWS, block=MOE_BLOCK, n_blocks=N_MOE_BLOCKS):
    top_v, top_i = lax.top_k(logits[:n_tok, :N_EXPERTS], TOP_K)
    gates = jax.nn.softmax(top_v, axis=-1)
    tk = n_tok * TOP_K
    flat_e = top_i.reshape(tk).astype(jnp.int32)
    order = jnp.argsort(flat_e)
    se = flat_e[order]
    counts = jnp.bincount(flat_e, length=N_EXPERTS)
    padded = (counts + block - 1) // block * block
    ends = jnp.cumsum(padded)
    dest_sorted = ((ends - padded)[se] + jnp.arange(tk) - (jnp.cumsum(counts) - counts)[se]).astype(jnp.int32)
    pad_tok = n_tok
    slot_tok = jnp.full((n_blocks * block,), pad_tok, jnp.int32).at[dest_sorted].set((order // TOP_K).astype(jnp.int32))
    slot_gate = jnp.zeros((n_blocks * block,), F32).at[dest_sorted].set(gates.reshape(tk)[order])
    block_exp = jnp.minimum(jnp.searchsorted(ends, jnp.arange(n_blocks) * block, side='right'),
                            N_EXPERTS - 1).astype(jnp.int32)
    n_used = (ends[-1] // block).astype(jnp.int32).reshape(1)
    zero_slot = (n_blocks - 1) * block
    dest = jnp.full((n_rows * TOP_K,), zero_slot, jnp.int32).at[order].set(dest_sorted)
    return slot_tok, slot_gate.reshape(-1, 1), block_exp, n_used, dest


def _reorder_w_in(w_in):
    segs, start = [], 0
    for w in IN_COLS:
        segs.append(w_in[..., start:start + w])
        start += w
    fq, fk, fv, ff, sq, sk, sv, hq, hf, hi, hg, ga, gb, gc = segs
    lead = w_in.shape[:-1]
    ff_pad = jnp.zeros(lead + (LANES - FOX_HEADS,), w_in.dtype)
    tail = jnp.zeros(lead + (Z_WIDTH - C_END,), w_in.dtype)
    return jnp.concatenate([ga, gb, gc, fq, sq, fk, fv, sk, sv, hq, hf, hi, hg, ff, ff_pad, tail],
                           axis=-1).astype(BF)


def _sample_q_rows(zs, col):
    q = zs[:, col:col + FOX_WIDTH].reshape(DEC_BATCH, DEC_SEQ, FOX_HEADS, HEAD_DIM)
    q = jnp.pad(q, ((0, 0), (0, 0), (0, Q_ROWS_PER_TOKEN - FOX_HEADS), (0, 0)))
    return q.reshape(DEC_BATCH, N_QROWS, HEAD_DIM)


def _sample_new_page(zs, col):
    k = zs[:, col:col + FOX_KV_WIDTH].astype(F32).reshape(DEC_BATCH, DEC_SEQ * FOX_KV_HEADS, HEAD_DIM)
    return jnp.pad(k, ((0, 0), (0, PAGE_COLS - DEC_SEQ * FOX_KV_HEADS), (0, 0)))


def _sample_o_rows(o):
    o = o.reshape(DEC_BATCH, DEC_SEQ, Q_ROWS_PER_TOKEN, HEAD_DIM)[:, :, :FOX_HEADS]
    return o.reshape(N_SAMPLE, FOX_WIDTH)


def kernel(x_prompt, x_sample, cache_fox_k, cache_fox_v, cache_fox_logf, cache_sb_k, cache_sb_v,
           state_hgrn, page_table, c_prompt, c_sample, ada_w, ada_b, norm_mix_g, w_in, fox_fgate_b,
           hgrn_lb_logits, hgrn_gnorm_g, w_branch_a, w_branch_b, w_branch_c, w_out, norm_ffn_g,
           router_w, router_b, moe_w1, moe_b1, moe_w2, moe_b2, final_norm_g):
    d = D_MODEL
    n_pool = cache_fox_k.shape[1]
    lb_sm = jax.nn.softmax(hgrn_lb_logits.astype(F32), axis=0)
    hgrn_lb = jnp.cumsum(lb_sm, axis=0) - lb_sm[0]
    w_in_r = _reorder_w_in(w_in)
    router_w_p = jnp.pad(router_w, ((0, 0), (0, 0), (0, LANES - N_EXPERTS)))
    router_b_p = jnp.pad(router_b, ((0, 0), (0, LANES - N_EXPERTS))).reshape(DEPTH, 1, LANES)
    cache_views = [c.reshape(DEPTH, n_pool, PAGE_COLS, HEAD_DIM)
                   for c in (cache_fox_k, cache_fox_v, cache_sb_k, cache_sb_v)]
    ck_fox, cv_fox, ck_sb, cv_sb = cache_views

    c_all = jnp.concatenate([c_prompt, c_sample, jnp.zeros((16 - BATCH - DEC_BATCH, d), F32)], axis=0)
    mod = adaln_mod(c_all, ada_w, ada_b)
    mod_p = mod.reshape(DEPTH, 16, 1, N_MOD * d)
    mod_s = jnp.repeat(mod[:, BATCH:BATCH + DEC_BATCH], DEC_SEQ, axis=1)

    x = jnp.concatenate([x_prompt.reshape(N_PROMPT, d), x_sample.reshape(N_SAMPLE, d),
                         jnp.zeros((R_ROWS - N_TOK, d), F32)], axis=0)
    u = norm_rows(x, norm_mix_g, layer=0, mod_p=mod_p, mod_s=mod_s, k_scale=1, k_shift=0)

    outs = {k: [] for k in ("pfk", "pfv", "pfl", "psk", "psv", "ph", "sfk", "sfv", "sfl", "ssk", "ssv", "sh")}
    y = None
    for l in range(DEPTH):
        z = matmul(u, w_in_r, layer=l, tm=MM_TM, tn=MM_TN, out_dtype=BF)
        zs = z[N_PROMPT:N_TOK]

        ff = z[:N_TOK, C_FF:C_FF + FOX_HEADS].astype(F32)
        logf = jax.nn.log_sigmoid(ff + fox_fgate_b[l].astype(F32))
        logf_p = logf[:N_PROMPT].reshape(BATCH, SEQ, FOX_HEADS)
        logf_s = logf[N_PROMPT:].reshape(DEC_BATCH, DEC_SEQ, FOX_HEADS)
        c_p = jnp.cumsum(logf_p, axis=1)
        c_pt = jnp.transpose(c_p, (0, 2, 1))
        past_logf = cache_fox_logf[l][page_table].astype(F32).reshape(DEC_BATCH, PAST_LEN, FOX_HEADS)
        c_s = jnp.cumsum(jnp.concatenate([past_logf, logf_s], axis=1), axis=1)

        oa_p = prompt_attention(z, kind="fox", c_q=c_pt[..., None],
                                c_k=c_pt.reshape(BATCH, FOX_HEADS, SEQ // 256, 1, 256),
                                col_q=C_FQ, col_k=C_FK, col_v=C_FV, tq=256)
        ob_p = prompt_attention(z, kind="sb", c_q=None, col_q=C_SQ, col_k=C_SK, col_v=C_SV)
        s0_p = jnp.zeros((BATCH, HGRN_HEADS, HGRN_VDIM, HGRN_EXPAND), F32)
        oc_p, st_p = hgrn_mixer(z, hgrn_lb, hgrn_gnorm_g, s0_p, layer=l, batch=BATCH, seq=SEQ,
                                col_q=C_HQ, col_f=C_HF, col_i=C_HI, col_g=C_HG)

        cq_s = jnp.pad(c_s[:, PAST_LEN:], ((0, 0), (0, 0), (0, Q_ROWS_PER_TOKEN - FOX_HEADS)))
        cq_s = cq_s.reshape(DEC_BATCH, N_QROWS, 1)
        ck_all = jnp.pad(c_s, ((0, 0), (0, PAGE_SIZE - DEC_SEQ), (0, Q_ROWS_PER_TOKEN - FOX_HEADS)))
        ck_all = ck_all.reshape(DEC_BATCH, N_PAGES + 1, PAGE_SIZE, Q_ROWS_PER_TOKEN)
        ck_all = jnp.repeat(jnp.transpose(ck_all, (0, 1, 3, 2)), FOX_KV_HEADS, axis=-1)
        oa_s = sample_attention(_sample_q_rows(zs, C_FQ), ck_fox, cv_fox, _sample_new_page(zs, C_FK),
                                _sample_new_page(zs, C_FV), page_table, kind="fox", layer=l,
                                c_q=cq_s, c_k=ck_all)
        ob_s = sample_attention(_sample_q_rows(zs, C_SQ), ck_sb, cv_sb, _sample_new_page(zs, C_SK),
                                _sample_new_page(zs, C_SV), page_table, kind="sb", layer=l)
        zh = zs[:, C_HQ:C_HG + HGRN_WIDTH].reshape(DEC_BATCH, DEC_SEQ, 4 * HGRN_WIDTH)
        zh = jnp.pad(zh, ((0, 0), (0, HGRN_CHUNK - DEC_SEQ), (0, 0))).reshape(DEC_BATCH * HGRN_CHUNK, 4 * HGRN_WIDTH)
        s0_s = jnp.swapaxes(state_hgrn[l].astype(F32), -1, -2)
        oc_s, st_s = hgrn_mixer(zh, hgrn_lb, hgrn_gnorm_g, s0_s, layer=l, batch=DEC_BATCH, seq=HGRN_CHUNK,
                                col_q=0, col_f=HGRN_FDIM, col_i=2 * HGRN_FDIM, col_g=2 * HGRN_FDIM + HGRN_WIDTH,
                                valid_len=DEC_SEQ)
        oc_s = oc_s.reshape(DEC_BATCH, HGRN_CHUNK, HGRN_WIDTH)[:, :DEC_SEQ].reshape(N_SAMPLE, HGRN_WIDTH)

        def rows(p, s):
            return jnp.concatenate([p, s.astype(BF), jnp.zeros((R_ROWS - N_TOK, p.shape[1]), BF)], axis=0)

        o_a = rows(oa_p, _sample_o_rows(oa_s))
        o_b = rows(ob_p, _sample_o_rows(ob_s))
        o_c = rows(oc_p, oc_s)

        merged = merge_branches(o_a, o_b, o_c, w_branch_a, w_branch_b, w_branch_c, z, layer=l)
        mix = matmul(merged, w_out, layer=l, tm=MM_TM, tn=MM_TN, out_dtype=F32)
        x, u2 = norm_rows(x, norm_ffn_g, layer=l, mod_p=mod_p, mod_s=mod_s, k_scale=4, k_shift=3,
                          delta=mix, k_gate=2, write_x=True, out_dtype=F32)

        logits = router_logits(u2, router_w_p, router_b_p, layer=l)
        slot_tok, slot_gate, block_exp, n_used, dest = moe_dispatch(logits)
        x_sorted = gather_rows(u2, slot_tok)
        act = moe_up(x_sorted, moe_w1, moe_b1, block_exp, n_used, layer=l)
        y_slots = moe_down(act, moe_w2, moe_b2, slot_gate, block_exp, n_used, layer=l)
        moe_out = combine_rows(y_slots, dest)
        if l + 1 < DEPTH:
            x, u = norm_rows(x, norm_mix_g, layer=l + 1, mod_p=mod_p, mod_s=mod_s, k_scale=1, k_shift=0,
                             delta=moe_out, k_gate=5, gate_layer=l, write_x=True)
        else:
            y = _final_norm(x, final_norm_g, moe_out, mod_p, mod_s, l)

        def kv(c0, r0, b, s):
            return z[r0:r0 + b * s, c0:c0 + FOX_KV_WIDTH].astype(F32).reshape(b, s, FOX_KV_HEADS, HEAD_DIM)

        outs["pfk"].append(kv(C_FK, 0, BATCH, SEQ))
        outs["pfv"].append(kv(C_FV, 0, BATCH, SEQ))
        outs["pfl"].append(logf_p)
        outs["psk"].append(kv(C_SK, 0, BATCH, SEQ))
        outs["psv"].append(kv(C_SV, 0, BATCH, SEQ))
        outs["ph"].append(jnp.swapaxes(st_p, -1, -2))
        outs["sfk"].append(kv(C_FK, N_PROMPT, DEC_BATCH, DEC_SEQ))
        outs["sfv"].append(kv(C_FV, N_PROMPT, DEC_BATCH, DEC_SEQ))
        outs["sfl"].append(logf_s)
        outs["ssk"].append(kv(C_SK, N_PROMPT, DEC_BATCH, DEC_SEQ))
        outs["ssv"].append(kv(C_SV, N_PROMPT, DEC_BATCH, DEC_SEQ))
        outs["sh"].append(jnp.swapaxes(st_s, -1, -2))

    st = {k: jnp.stack(v) for k, v in outs.items()}
    y_prompt = y[:N_PROMPT].reshape(BATCH, SEQ, d)
    y_sample = y[N_PROMPT:N_TOK].reshape(DEC_BATCH, DEC_SEQ, d)
    return (y_prompt, y_sample, st["pfk"], st["pfv"], st["pfl"], st["psk"], st["psv"], st["ph"],
            st["sfk"], st["sfv"], st["sfl"], st["ssk"], st["ssv"], st["sh"])


def _final_norm(x, g, delta, mod_p, mod_s, layer):
    return norm_rows(x, g, layer=layer, mod_p=mod_p, mod_s=mod_s, delta=delta, k_gate=5, out_dtype=F32)
```

```python
import functools

import jax
import jax.numpy as jnp
from jax import lax
from jax.experimental import pallas as pl
from jax.experimental.pallas import tpu as pltpu

BF = jnp.bfloat16
F32 = jnp.float32

D_MODEL = 4096
BATCH = 4
SEQ = 2048
DEPTH = 4
DEC_BATCH = 8
DEC_SEQ = 4
PAST_LEN = 8192
PAGE_SIZE = 128
HEAD_DIM = 128
FOX_HEADS = 12
FOX_KV_HEADS = 4
SB_HEADS = 12
SB_KV_HEADS = 4
GROUP = 3
HGRN_HEADS = 8
HGRN_EXPAND = 128
HGRN_VDIM = 128
HGRN_CHUNK = 64
HGRN_SUB = 16
N_EXPERTS = 32
TOP_K = 4
D_EXPERT = D_MODEL // 4
SWIGLU_LIMIT = 7.0
SWIGLU_ALPHA = 1.702
NORM_EPS = 1e-6
NEG_BIG = -1e30
N_MOD = 6
ATT_SCALE = HEAD_DIM ** -0.5

FOX_WIDTH = FOX_HEADS * HEAD_DIM
FOX_KV_WIDTH = FOX_KV_HEADS * HEAD_DIM
SB_WIDTH = SB_HEADS * HEAD_DIM
SB_KV_WIDTH = SB_KV_HEADS * HEAD_DIM
HGRN_FDIM = HGRN_HEADS * HGRN_EXPAND
HGRN_WIDTH = HGRN_HEADS * HGRN_VDIM
IN_COLS = (FOX_WIDTH, FOX_KV_WIDTH, FOX_KV_WIDTH, FOX_HEADS,
           SB_WIDTH, SB_KV_WIDTH, SB_KV_WIDTH,
           HGRN_FDIM, HGRN_FDIM, HGRN_WIDTH, HGRN_WIDTH,
           D_MODEL, D_MODEL, D_MODEL)

LANES = 128
VMEM_LIMIT_BYTES = 56 * 1024 * 1024

ROW_TILE = 256
N_PROMPT = BATCH * SEQ
N_SAMPLE = DEC_BATCH * DEC_SEQ
N_TOK = N_PROMPT + N_SAMPLE
R_ROWS = N_PROMPT + ROW_TILE
MM_TM = 768
MM_TN = 512

W_Z1 = FOX_WIDTH + 2 * FOX_KV_WIDTH
W_SHIFT = FOX_HEADS
W_Z2 = SB_WIDTH + 2 * SB_KV_WIDTH + 2 * HGRN_FDIM + 2 * HGRN_WIDTH + 3 * D_MODEL
Z1_FQ = 0
Z1_FK = Z1_FQ + FOX_WIDTH
Z1_FV = Z1_FK + FOX_KV_WIDTH
Z2_SQ = 0
Z2_SK = Z2_SQ + SB_WIDTH
Z2_SV = Z2_SK + SB_KV_WIDTH
Z2_HQ = Z2_SV + SB_KV_WIDTH
Z2_HF = Z2_HQ + HGRN_FDIM
Z2_HI = Z2_HF + HGRN_FDIM
Z2_HG = Z2_HI + HGRN_WIDTH
Z2_GA = Z2_HG + HGRN_WIDTH
Z2_GB = Z2_GA + D_MODEL
Z2_GC = Z2_GB + D_MODEL
assert W_Z1 % MM_TN == 0 and W_Z2 % MM_TN == 0 and Z2_GA % MM_TN == 0 and D_MODEL % MM_TN == 0

MOE_BLOCK = 384
N_SLOT_TOK = N_TOK * TOP_K
N_MOE_BLOCKS = -(-N_SLOT_TOK // MOE_BLOCK) + N_EXPERTS
N_SLOTS = N_MOE_BLOCKS * MOE_BLOCK


def _params(sem):
    return pltpu.CompilerParams(dimension_semantics=sem, vmem_limit_bytes=VMEM_LIMIT_BYTES)


def _mm_kernel(x_ref, w_ref, o_ref, wbf_ref):
    @pl.when(pl.program_id(1) == 0)
    def _():
        wbf_ref[...] = w_ref[...].astype(BF)

    o_ref[...] = jnp.dot(x_ref[...], wbf_ref[...], preferred_element_type=F32).astype(o_ref.dtype)


def matmul(x, w, *, layer, tm, tn, out_dtype, col_block0=0, n_col_blocks=None):
    m, k = x.shape
    if n_col_blocks is None:
        n_col_blocks = w.shape[-1] // tn
    assert m % tm == 0
    return pl.pallas_call(
        _mm_kernel,
        grid=(n_col_blocks, m // tm),
        in_specs=[pl.BlockSpec((tm, k), lambda j, i: (i, 0)),
                  pl.BlockSpec((None, k, tn), lambda j, i: (layer, 0, col_block0 + j))],
        out_specs=pl.BlockSpec((tm, tn), lambda j, i: (i, j)),
        out_shape=jax.ShapeDtypeStruct((m, n_col_blocks * tn), out_dtype),
        scratch_shapes=[pltpu.VMEM((k, tn), BF)],
        compiler_params=_params(("arbitrary", "arbitrary")),
        name="dense_matmul",
    )(x, w)


SHIFT_ROWS = 512


def _mm_shift_kernel(x_ref, wa_ref, wb_ref, we_ref, o_ref, wbf_ref, *, shift, tn):
    j = pl.program_id(0)

    @pl.when(pl.program_id(1) == 0)
    def _():
        wide = tn + LANES
        sel = (lax.broadcasted_iota(jnp.int32, (wide, tn), 0)
               == lax.broadcasted_iota(jnp.int32, (wide, tn), 1) + shift).astype(BF)
        lane = lax.broadcasted_iota(jnp.int32, (SHIFT_ROWS, LANES), 1)
        last = j == pl.num_programs(0) - 1
        for r0 in range(0, wbf_ref.shape[0], SHIFT_ROWS):
            rows = pl.ds(r0, SHIFT_ROWS)
            nb = jnp.where(last, we_ref[rows, :], wb_ref[rows, :])
            nb = jnp.where(lane < shift, nb, 0.0)
            cat = jnp.concatenate([wa_ref[rows, :].astype(BF), nb.astype(BF)], axis=1)
            wbf_ref[rows, :] = jnp.dot(cat, sel, preferred_element_type=F32).astype(BF)

    o_ref[...] = jnp.dot(x_ref[...], wbf_ref[...], preferred_element_type=F32).astype(o_ref.dtype)


def matmul_shifted(x, w, w_edge, *, layer, col0, shift, width, tm, tn, out_dtype):
    m, k = x.shape
    n_tiles = width // tn
    assert m % tm == 0 and width % tn == 0 and col0 % tn == 0 and k % SHIFT_ROWS == 0
    assert col0 + shift + width == w.shape[-1]
    wide0 = col0 // tn
    per = tn // LANES
    narrow_last = (col0 + width) // LANES - 1
    return pl.pallas_call(
        functools.partial(_mm_shift_kernel, shift=shift, tn=tn),
        grid=(n_tiles, m // tm),
        in_specs=[pl.BlockSpec((tm, k), lambda j, i: (i, 0)),
                  pl.BlockSpec((None, k, tn), lambda j, i: (layer, 0, wide0 + j)),
                  pl.BlockSpec((None, k, LANES),
                               lambda j, i: (layer, 0, jnp.minimum((wide0 + j + 1) * per, narrow_last))),
                  pl.BlockSpec((None, k, LANES), lambda j, i: (layer, 0, 0))],
        out_specs=pl.BlockSpec((tm, tn), lambda j, i: (i, j)),
        out_shape=jax.ShapeDtypeStruct((m, width), out_dtype),
        scratch_shapes=[pltpu.VMEM((k, tn), BF)],
        compiler_params=_params(("arbitrary", "arbitrary")),
        name="dense_matmul_shifted",
    )(x, w, w, w_edge)


def _ada_kernel(c_ref, w_ref, b_ref, o_ref):
    c = c_ref[...]
    s = (c * jax.nn.sigmoid(c)).astype(BF)
    acc = jnp.dot(s, w_ref[...].astype(BF), preferred_element_type=F32)
    o_ref[...] = acc + b_ref[...]


def adaln_mod(c_all, ada_w, ada_b, *, tn=1024):
    depth, d, n = ada_w.shape
    rows = c_all.shape[0]
    return pl.pallas_call(
        _ada_kernel,
        grid=(depth, n // tn),
        in_specs=[pl.BlockSpec((rows, d), lambda l, j: (0, 0)),
                  pl.BlockSpec((None, d, tn), lambda l, j: (l, 0, j)),
                  pl.BlockSpec((None, 1, tn), lambda l, j: (l, 0, j))],
        out_specs=pl.BlockSpec((None, rows, tn), lambda l, j: (l, 0, j)),
        out_shape=jax.ShapeDtypeStruct((depth, rows, n), F32),
        compiler_params=_params(("arbitrary", "arbitrary")),
        name="adaln_mod",
    )(c_all, ada_w, ada_b.reshape(depth, 1, n))


def _norm_kernel(*refs, has_delta, modulate, write_x, n_prompt_blocks, n_sample):
    it = iter(refs)
    x_ref = next(it)
    if has_delta:
        d_ref, gtp_ref, gts_ref = next(it), next(it), next(it)
    g_ref = next(it)
    if modulate:
        scp_ref, shp_ref, scs_ref, shs_ref = next(it), next(it), next(it), next(it)
    if write_x:
        xo_ref = next(it)
    u_ref = next(it)
    i = pl.program_id(0)

    def body(x, d, gt, sc, sh):
        if has_delta:
            x = x + gt * d
        ms = jnp.mean(x * x, axis=-1, keepdims=True)
        y = x * lax.rsqrt(ms + NORM_EPS) * g_ref[...]
        if modulate:
            y = y * (1.0 + sc) + sh
        return x, y

    @pl.when(i < n_prompt_blocks)
    def _():
        x, y = body(x_ref[...],
                    d_ref[...] if has_delta else None,
                    gtp_ref[...] if has_delta else None,
                    scp_ref[...] if modulate else None,
                    shp_ref[...] if modulate else None)
        if write_x:
            xo_ref[...] = x
        u_ref[...] = y.astype(u_ref.dtype)

    @pl.when(i >= n_prompt_blocks)
    def _():
        rows = pl.ds(0, n_sample)
        x, y = body(x_ref[rows, :],
                    d_ref[rows, :] if has_delta else None,
                    gts_ref[...] if has_delta else None,
                    scs_ref[...] if modulate else None,
                    shs_ref[...] if modulate else None)
        if write_x:
            xo_ref[...] = jnp.zeros_like(xo_ref)
            xo_ref[rows, :] = x
        u_ref[...] = jnp.zeros_like(u_ref)
        u_ref[rows, :] = y.astype(u_ref.dtype)


def norm_rows(x, g, *, layer, mod_p=None, mod_s=None, k_scale=None, k_shift=None,
              delta=None, k_gate=None, gate_layer=None, write_x=False, out_dtype=BF, rows_per_batch=SEQ,
              n_prompt=N_PROMPT, n_sample=N_SAMPLE, tile=ROW_TILE):
    if gate_layer is None:
        gate_layer = layer
    r, d = x.shape
    npb = n_prompt // tile
    bpb = rows_per_batch // tile
    has_delta = delta is not None
    modulate = k_scale is not None
    row_spec = pl.BlockSpec((tile, d), lambda i: (i, 0))

    def pspec(k, lyr=layer):
        return pl.BlockSpec((None, None, 1, d), lambda i: (lyr, jnp.minimum(i, npb - 1) // bpb, 0, k))

    def sspec(k, lyr=layer):
        return pl.BlockSpec((None, n_sample, d), lambda i: (lyr, 0, k))

    args, specs = [x], [row_spec]
    if has_delta:
        args += [delta, mod_p, mod_s]
        specs += [row_spec, pspec(k_gate, gate_layer), sspec(k_gate, gate_layer)]
    if g.ndim == 2:
        args.append(g.reshape(g.shape[0], 1, d))
        specs.append(pl.BlockSpec((None, 1, d), lambda i: (layer, 0, 0)))
    else:
        args.append(g.reshape(1, d))
        specs.append(pl.BlockSpec((1, d), lambda i: (0, 0)))
    if modulate:
        args += [mod_p, mod_p, mod_s, mod_s]
        specs += [pspec(k_scale), pspec(k_shift), sspec(k_scale), sspec(k_shift)]
    out_shape, out_specs = [], []
    if write_x:
        out_shape.append(jax.ShapeDtypeStruct((r, d), F32))
        out_specs.append(row_spec)
    out_shape.append(jax.ShapeDtypeStruct((r, d), out_dtype))
    out_specs.append(row_spec)
    res = pl.pallas_call(
        functools.partial(_norm_kernel, has_delta=has_delta, modulate=modulate, write_x=write_x,
                          n_prompt_blocks=npb, n_sample=n_sample),
        grid=(r // tile,),
        in_specs=specs,
        out_specs=out_specs,
        out_shape=out_shape,
        compiler_params=_params(("arbitrary",)),
        name="resid_norm",
    )(*args)
    return res if write_x else res[0]


def _log_sigmoid(x):
    return jnp.minimum(x, 0.0) - jnp.log1p(jnp.exp(-jnp.abs(x)))


def _split_hi_lo(x):
    hi = x.astype(BF)
    lo = (x - hi.astype(F32)).astype(BF)
    return hi, lo


def _fox_prompt_kernel(q_ref, k_ref, v_ref, cq_ref, ck_ref, o_ref, m_ref, l_ref, acc_ref, *, tq, group):
    qi = pl.program_id(2)
    m_ref[...] = jnp.full_like(m_ref, NEG_BIG)
    l_ref[...] = jnp.zeros_like(l_ref)
    acc_ref[...] = jnp.zeros_like(acc_ref)
    causal = lax.broadcasted_iota(jnp.int32, (tq, tq), 1) <= lax.broadcasted_iota(jnp.int32, (tq, tq), 0)

    def tile(j, diagonal):
        ks = pl.multiple_of(j * tq, tq)
        k = k_ref[pl.ds(ks, tq), :]
        v = v_ref[pl.ds(ks, tq), :]
        for g in range(group):
            q = q_ref[:, g * HEAD_DIM:(g + 1) * HEAD_DIM]
            s = lax.dot_general(q, k, (((1,), (1,)), ((), ())), preferred_element_type=F32)
            s = s * ATT_SCALE + (cq_ref[g] - ck_ref[g, j])
            if diagonal:
                s = jnp.where(causal, s, NEG_BIG)
            m = m_ref[g]
            m_new = jnp.maximum(m, jnp.max(s, axis=-1, keepdims=True))
            a = jnp.exp(m - m_new)
            p = jnp.exp(s - m_new)
            l_ref[g] = a * l_ref[g] + jnp.sum(p, axis=-1, keepdims=True)
            acc_ref[g] = a * acc_ref[g] + jnp.dot(p.astype(BF), v, preferred_element_type=F32)
            m_ref[g] = m_new

    def step(j, carry):
        tile(j, False)
        return carry

    lax.fori_loop(0, qi, step, 0)
    tile(qi, True)
    for g in range(group):
        o_ref[:, g * HEAD_DIM:(g + 1) * HEAD_DIM] = (acc_ref[g] / l_ref[g]).astype(o_ref.dtype)


def _sb_prompt_kernel(q_ref, k_ref, v_ref, o_ref, later_ref, acc_ref, *, tq, group):
    qi = pl.program_id(2)
    later_ref[...] = jnp.zeros_like(later_ref)
    acc_ref[...] = jnp.zeros_like(acc_ref)
    r_io = lax.broadcasted_iota(jnp.int32, (tq, tq), 0)
    c_io = lax.broadcasted_iota(jnp.int32, (tq, tq), 1)
    strict = c_io < r_io
    upper = (r_io > c_io).astype(BF)

    def tile(j, diagonal):
        ks = pl.multiple_of(j * tq, tq)
        k = k_ref[pl.ds(ks, tq), :]
        v = v_ref[pl.ds(ks, tq), :]
        for g in range(group):
            q = q_ref[:, g * HEAD_DIM:(g + 1) * HEAD_DIM]
            z = lax.dot_general(q, k, (((1,), (1,)), ((), ())), preferred_element_type=F32) * ATT_SCALE
            ln = _log_sigmoid(-z)
            if diagonal:
                ln = jnp.where(strict, ln, 0.0)
            hi, lo = _split_hi_lo(ln)
            after = (jnp.dot(hi, upper, preferred_element_type=F32)
                     + jnp.dot(lo, upper, preferred_element_type=F32)) + later_ref[g]
            a = jnp.exp(z + ln + after)
            if diagonal:
                a = jnp.where(strict, a, 0.0)
            acc_ref[g] += jnp.dot(a.astype(BF), v, preferred_element_type=F32)
            later_ref[g] += jnp.sum(ln, axis=-1, keepdims=True)

    tile(qi, True)

    def step(jj, carry):
        tile(qi - 1 - jj, False)
        return carry

    lax.fori_loop(0, qi, step, 0)
    for g in range(group):
        o_ref[:, g * HEAD_DIM:(g + 1) * HEAD_DIM] = acc_ref[g].astype(o_ref.dtype)


def prompt_attention(z, *, kind, c_q, c_k=None, batch=BATCH, seq=SEQ, kv_heads=FOX_KV_HEADS,
                     group=GROUP, col_q, col_k, col_v, tq=256):
    qw = group * HEAD_DIM
    nq = seq // tq
    assert col_q % qw == 0 and col_k % HEAD_DIM == 0 and col_v % HEAD_DIM == 0
    in_specs = [
        pl.BlockSpec((tq, qw), lambda b, h, qi: (b * nq + qi, col_q // qw + h)),
        pl.BlockSpec((seq, HEAD_DIM), lambda b, h, qi: (b, col_k // HEAD_DIM + h)),
        pl.BlockSpec((seq, HEAD_DIM), lambda b, h, qi: (b, col_v // HEAD_DIM + h)),
    ]
    args = [z, z, z]
    if kind == "fox":
        in_specs += [pl.BlockSpec((None, group, tq, 1), lambda b, h, qi: (b, h, qi, 0)),
                     pl.BlockSpec((None, group, nq, 1, tq), lambda b, h, qi: (b, h, 0, 0, 0))]
        args += [c_q, c_k]
        kern = functools.partial(_fox_prompt_kernel, tq=tq, group=group)
        scratch = [pltpu.VMEM((group, tq, 1), F32), pltpu.VMEM((group, tq, 1), F32),
                   pltpu.VMEM((group, tq, HEAD_DIM), F32)]
    else:
        kern = functools.partial(_sb_prompt_kernel, tq=tq, group=group)
        scratch = [pltpu.VMEM((group, tq, 1), F32), pltpu.VMEM((group, tq, HEAD_DIM), F32)]
    return pl.pallas_call(
        kern,
        grid=(batch, kv_heads, nq),
        in_specs=in_specs,
        out_specs=pl.BlockSpec((tq, qw), lambda b, h, qi: (b * nq + qi, h)),
        out_shape=jax.ShapeDtypeStruct((batch * seq, kv_heads * qw), BF),
        scratch_shapes=scratch,
        compiler_params=_params(("arbitrary", "arbitrary", "arbitrary")),
        name=kind + "_prompt_attention",
    )(*args)


Q_ROWS_PER_TOKEN = 16
N_QROWS = DEC_SEQ * Q_ROWS_PER_TOKEN
PAGE_COLS = PAGE_SIZE * FOX_KV_HEADS
N_PAGES = PAST_LEN // PAGE_SIZE


def _sample_masks(new_step, strict):
    r = lax.broadcasted_iota(jnp.int32, (N_QROWS, PAGE_COLS), 0)
    c = lax.broadcasted_iota(jnp.int32, (N_QROWS, PAGE_COLS), 1)
    head = r % Q_ROWS_PER_TOKEN
    t = r // Q_ROWS_PER_TOKEN
    pos = c // FOX_KV_HEADS
    ok = (head < FOX_HEADS) & ((c % FOX_KV_HEADS) == head // GROUP)
    causal_new = (pos < t) if strict else (pos <= t)
    return ok & (jnp.logical_not(new_step) | causal_new)


def _fox_sample_kernel(pt_ref, q_ref, kc_ref, vc_ref, kn_ref, vn_ref, cq_ref, ck_ref, o_ref,
                       m_ref, l_ref, acc_ref, *, n_pages):
    p = pl.program_id(1)
    new_step = p == n_pages

    @pl.when(p == 0)
    def _():
        m_ref[...] = jnp.full_like(m_ref, NEG_BIG)
        l_ref[...] = jnp.zeros_like(l_ref)
        acc_ref[...] = jnp.zeros_like(acc_ref)

    k = jnp.where(new_step, kn_ref[...], kc_ref[...]).astype(BF)
    v = jnp.where(new_step, vn_ref[...], vc_ref[...]).astype(BF)
    s = lax.dot_general(q_ref[...], k, (((1,), (1,)), ((), ())), preferred_element_type=F32)
    ck = ck_ref[...]
    bias = cq_ref[...] - jnp.concatenate([ck] * DEC_SEQ, axis=0)
    s = jnp.where(_sample_masks(new_step, False), s * ATT_SCALE + bias, NEG_BIG)
    m = m_ref[...]
    m_new = jnp.maximum(m, jnp.max(s, axis=-1, keepdims=True))
    a = jnp.exp(m - m_new)
    pr = jnp.where(s > 0.5 * NEG_BIG, jnp.exp(s - m_new), 0.0)
    l_ref[...] = a * l_ref[...] + jnp.sum(pr, axis=-1, keepdims=True)
    acc_ref[...] = a * acc_ref[...] + jnp.dot(pr.astype(BF), v, preferred_element_type=F32)
    m_ref[...] = m_new

    @pl.when(new_step)
    def _():
        l = l_ref[...]
        o_ref[...] = acc_ref[...] / jnp.where(l > 0.0, l, 1.0)


def _sb_sample_kernel(pt_ref, q_ref, kc_ref, vc_ref, kn_ref, vn_ref, o_ref, later_ref, acc_ref):
    p = pl.program_id(1)
    new_step = p == 0

    @pl.when(p == 0)
    def _():
        later_ref[...] = jnp.zeros_like(later_ref)
        acc_ref[...] = jnp.zeros_like(acc_ref)

    k = jnp.where(new_step, kn_ref[...], kc_ref[...]).astype(BF)
    v = jnp.where(new_step, vn_ref[...], vc_ref[...]).astype(BF)
    z = lax.dot_general(q_ref[...], k, (((1,), (1,)), ((), ())), preferred_element_type=F32) * ATT_SCALE
    mask = _sample_masks(new_step, True)
    ln = jnp.where(mask, _log_sigmoid(-z), 0.0)
    ci = lax.broadcasted_iota(jnp.int32, (PAGE_COLS, PAGE_COLS), 0) // FOX_KV_HEADS
    cj = lax.broadcasted_iota(jnp.int32, (PAGE_COLS, PAGE_COLS), 1) // FOX_KV_HEADS
    upper = (ci > cj).astype(BF)
    hi, lo = _split_hi_lo(ln)
    after = (jnp.dot(hi, upper, preferred_element_type=F32)
             + jnp.dot(lo, upper, preferred_element_type=F32)) + later_ref[...]
    a = jnp.where(mask, jnp.exp(_log_sigmoid(z) + after), 0.0)
    acc_ref[...] += jnp.dot(a.astype(BF), v, preferred_element_type=F32)
    later_ref[...] += jnp.sum(ln, axis=-1, keepdims=True)

    @pl.when(p == pl.num_programs(1) - 1)
    def _():
        o_ref[...] = acc_ref[...]


def sample_attention(q_rows, cache_k, cache_v, k_new, v_new, page_table, *, kind, layer,
                     c_q=None, c_k=None):
    db = q_rows.shape[0]
    n_pages = page_table.shape[1]
    if kind == "fox":
        def page_of(p, pt, b):
            return pt[b, jnp.minimum(p, n_pages - 1)]
    else:
        def page_of(p, pt, b):
            return pt[b, n_pages - jnp.maximum(p, 1)]
    q_spec = pl.BlockSpec((None, N_QROWS, HEAD_DIM), lambda b, p, pt: (b, 0, 0))
    c_spec = pl.BlockSpec((None, None, PAGE_COLS, HEAD_DIM), lambda b, p, pt: (layer, page_of(p, pt, b), 0, 0))
    n_spec = pl.BlockSpec((None, PAGE_COLS, HEAD_DIM), lambda b, p, pt: (b, 0, 0))
    o_spec = pl.BlockSpec((None, N_QROWS, HEAD_DIM), lambda b, p, pt: (b, 0, 0))
    in_specs = [q_spec, c_spec, c_spec, n_spec, n_spec]
    args = [q_rows, cache_k, cache_v, k_new, v_new]
    if kind == "fox":
        in_specs += [pl.BlockSpec((None, N_QROWS, 1), lambda b, p, pt: (b, 0, 0)),
                     pl.BlockSpec((None, None, Q_ROWS_PER_TOKEN, PAGE_COLS), lambda b, p, pt: (b, p, 0, 0))]
        args += [c_q, c_k]
        kern = functools.partial(_fox_sample_kernel, n_pages=n_pages)
        scratch = [pltpu.VMEM((N_QROWS, 1), F32), pltpu.VMEM((N_QROWS, 1), F32), pltpu.VMEM((N_QROWS, HEAD_DIM), F32)]
    else:
        kern = _sb_sample_kernel
        scratch = [pltpu.VMEM((N_QROWS, 1), F32), pltpu.VMEM((N_QROWS, HEAD_DIM), F32)]
    return pl.pallas_call(
        kern,
        grid_spec=pltpu.PrefetchScalarGridSpec(
            num_scalar_prefetch=1, grid=(db, n_pages + 1), in_specs=in_specs, out_specs=o_spec,
            scratch_shapes=scratch),
        out_shape=jax.ShapeDtypeStruct((db, N_QROWS, HEAD_DIM), F32),
        compiler_params=_params(("arbitrary", "arbitrary")),
        name=kind + "_sample_attention",
    )(page_table, *args)


def _hgrn_kernel(q_ref, f_ref, i_ref, g_ref, lb_ref, gn_ref, s0_ref, o_ref, sfin_ref, st_ref,
                 *, n_chunks, valid_len):
    c, sub = HGRN_CHUNK, HGRN_SUB
    ns = c // sub
    lb = lb_ref[...]
    lb_pos = lb > 0.0
    log_lb = jnp.log(jnp.where(lb_pos, lb, 1.0))
    log_1m = jnp.log1p(-lb)
    tri = (lax.broadcasted_iota(jnp.int32, (c, c), 0) >= lax.broadcasted_iota(jnp.int32, (c, c), 1)).astype(BF)
    blk_col = lax.broadcasted_iota(jnp.int32, (sub, c), 1)
    sub_row = lax.broadcasted_iota(jnp.int32, (sub, 1), 0)
    st_ref[...] = s0_ref[...]

    def chunk(ci, _):
        r0 = pl.multiple_of(ci * c, c)
        x = f_ref[pl.ds(r0, c), :].astype(F32)
        ls = _log_sigmoid(x)
        a = log_1m + ls
        mx = jnp.maximum(log_lb, a)
        lae = mx + jnp.log1p(jnp.exp(-jnp.abs(log_lb - a)))
        logf = jnp.where(lb_pos, lae, ls)
        kk = (1.0 - lb) * jax.nn.sigmoid(-x)
        if valid_len is not None:
            live = (ci * c + lax.broadcasted_iota(jnp.int32, (c, 1), 0)) < valid_len
            logf = jnp.where(live, logf, 0.0)
            kk = jnp.where(live, kk, 0.0)
        qq = q_ref[pl.ds(r0, c), :].astype(F32) * (HGRN_EXPAND ** -0.5)
        vv = i_ref[pl.ds(r0, c), :]
        hi, lo = _split_hi_lo(logf)
        gcum = jnp.dot(tri, hi, preferred_element_type=F32) + jnp.dot(tri, lo, preferred_element_type=F32)
        g_end = gcum[c - 1:c, :]
        st = st_ref[...]
        o = lax.dot_general((qq * jnp.exp(gcum)).astype(BF), st.astype(BF), (((1,), (1,)), ((), ())),
                            preferred_element_type=F32)
        a_rows = []
        for i in range(ns):
            gi = gcum[i * sub:(i + 1) * sub, :]
            qi_ = qq[i * sub:(i + 1) * sub, :]
            ki_ = kk[i * sub:(i + 1) * sub, :]
            if i > 0:
                ref = gcum[i * sub - 1:i * sub, :]
                qt = (qi_ * jnp.exp(gi - ref)).astype(BF)
                kt = (kk * jnp.exp(jnp.minimum(ref - gcum, 0.0))).astype(BF)
                off = lax.dot_general(qt, kt, (((1,), (1,)), ((), ())), preferred_element_type=F32)
                a_i = jnp.where(blk_col < i * sub, off, 0.0)
            else:
                a_i = jnp.zeros((sub, c), F32)
            for s in range(sub):
                dec = jnp.exp(jnp.where(sub_row >= s, gi - gi[s:s + 1, :], 0.0))
                colv = jnp.sum(qi_ * ki_[s:s + 1, :] * dec, axis=-1, keepdims=True)
                a_i = jnp.where((blk_col == i * sub + s) & (sub_row >= s), colv, a_i)
            a_rows.append(a_i)
        amat = jnp.concatenate(a_rows, axis=0)
        o = o + jnp.dot(amat.astype(BF), vv, preferred_element_type=F32)
        kd = (kk * jnp.exp(g_end - gcum)).astype(BF)
        vt = jnp.transpose(vv.astype(F32)).astype(BF)
        st_ref[...] = st * jnp.exp(g_end) + jnp.dot(vt, kd, preferred_element_type=F32)
        o = o * lax.rsqrt(jnp.mean(o * o, axis=-1, keepdims=True) + NORM_EPS)
        hg = g_ref[pl.ds(r0, c), :].astype(F32)
        o = o * gn_ref[...] * (hg * jax.nn.sigmoid(hg))
        o_ref[pl.ds(r0, c), :] = o.astype(o_ref.dtype)
        return 0

    lax.fori_loop(0, n_chunks, chunk, 0)
    sfin_ref[...] = st_ref[...]


def hgrn_mixer(z, lb, gnorm, s0_t, *, layer, batch, seq, col_q, col_f, col_i, col_g, valid_len=None,
               heads=HGRN_HEADS):
    n = HGRN_EXPAND

    def col(c0):
        return pl.BlockSpec((seq, n), lambda b, h: (b, c0 // n + h))

    return pl.pallas_call(
        functools.partial(_hgrn_kernel, n_chunks=seq // HGRN_CHUNK, valid_len=valid_len),
        grid=(batch, heads),
        in_specs=[col(col_q), col(col_f), col(col_i), col(col_g),
                  pl.BlockSpec((None, 1, n), lambda b, h: (layer, 0, h)),
                  pl.BlockSpec((None, 1, n), lambda b, h: (layer, 0, 0)),
                  pl.BlockSpec((None, None, n, n), lambda b, h: (b, h, 0, 0))],
        out_specs=[pl.BlockSpec((seq, n), lambda b, h: (b, h)),
                   pl.BlockSpec((None, None, n, n), lambda b, h: (b, h, 0, 0))],
        out_shape=[jax.ShapeDtypeStruct((batch * seq, heads * n), BF),
                   jax.ShapeDtypeStruct((batch, heads, n, n), F32)],
        scratch_shapes=[pltpu.VMEM((n, n), F32)],
        compiler_params=_params(("arbitrary", "arbitrary")),
        name="hgrn_mixer",
    )(z, z, z, z, lb.reshape(lb.shape[0], 1, -1), gnorm.reshape(gnorm.shape[0], 1, -1), s0_t)


def _merge_kernel(oa_ref, ob_ref, oc_ref, wa_ref, wb_ref, wc_ref, ga_ref, gb_ref, gc_ref, o_ref,
                  wa_bf, wb_bf, wc_bf):
    @pl.when(pl.program_id(1) == 0)
    def _():
        wa_bf[...] = wa_ref[...].astype(BF)
        wb_bf[...] = wb_ref[...].astype(BF)
        wc_bf[...] = wc_ref[...].astype(BF)

    ya = jnp.dot(oa_ref[...], wa_bf[...], preferred_element_type=F32)
    yb = jnp.dot(ob_ref[...], wb_bf[...], preferred_element_type=F32)
    yc = jnp.dot(oc_ref[...], wc_bf[...], preferred_element_type=F32)
    y = (jax.nn.sigmoid(ga_ref[...].astype(F32)) * ya + jax.nn.sigmoid(gb_ref[...].astype(F32)) * yb
         + jax.nn.sigmoid(gc_ref[...].astype(F32)) * yc)
    o_ref[...] = y.astype(o_ref.dtype)


def merge_branches(o_a, o_b, o_c, w_a, w_b, w_c, z, *, layer, tm=MM_TM, tn=MM_TN):
    r = o_a.shape[0]
    d = w_a.shape[-1]

    def lhs(a):
        return pl.BlockSpec((tm, a.shape[1]), lambda j, i: (i, 0))

    def wsp(w):
        return pl.BlockSpec((None, w.shape[1], tn), lambda j, i: (layer, 0, j))

    def gate(c0):
        return pl.BlockSpec((tm, tn), lambda j, i: (i, c0 // tn + j))

    return pl.pallas_call(
        _merge_kernel,
        grid=(d // tn, r // tm),
        in_specs=[lhs(o_a), lhs(o_b), lhs(o_c), wsp(w_a), wsp(w_b), wsp(w_c),
                  gate(Z2_GA), gate(Z2_GB), gate(Z2_GC)],
        out_specs=pl.BlockSpec((tm, tn), lambda j, i: (i, j)),
        out_shape=jax.ShapeDtypeStruct((r, d), BF),
        scratch_shapes=[pltpu.VMEM((w_a.shape[1], tn), BF), pltpu.VMEM((w_b.shape[1], tn), BF),
                        pltpu.VMEM((w_c.shape[1], tn), BF)],
        compiler_params=_params(("arbitrary", "arbitrary")),
        name="merge_branches",
    )(o_a, o_b, o_c, w_a, w_b, w_c, z, z, z)


def _router_kernel(x_ref, w_ref, b_ref, o_ref):
    xh, xl = _split_hi_lo(x_ref[...])
    wh, wl = _split_hi_lo(w_ref[...])
    acc = jnp.dot(xh, wh, preferred_element_type=F32)
    acc = acc + jnp.dot(xh, wl, preferred_element_type=F32)
    acc = acc + jnp.dot(xl, wh, preferred_element_type=F32)
    o_ref[...] = acc + b_ref[...]


def router_logits(u, router_w, router_b, *, layer, tile=ROW_TILE):
    r, d = u.shape
    return pl.pallas_call(
        _router_kernel,
        grid=(r // tile,),
        in_specs=[pl.BlockSpec((tile, d), lambda i: (i, 0)),
                  pl.BlockSpec((None, d, LANES), lambda i: (layer, 0, 0)),
                  pl.BlockSpec((None, 1, LANES), lambda i: (layer, 0, 0))],
        out_specs=pl.BlockSpec((tile, LANES), lambda i: (i, 0)),
        out_shape=jax.ShapeDtypeStruct((r, LANES), F32),
        compiler_params=_params(("arbitrary",)),
        name="router_logits",
    )(u, router_w, router_b)


DMA_UNROLL = 8


def _run_row_copies(n, make_copy):
    assert n % DMA_UNROLL == 0

    def start(i, carry):
        for u in range(DMA_UNROLL):
            make_copy(i * DMA_UNROLL + u).start(priority=u % 2)
        return carry

    def wait(i, carry):
        for u in range(DMA_UNROLL):
            make_copy(i * DMA_UNROLL + u).wait()
        return carry

    lax.fori_loop(0, n // DMA_UNROLL, start, 0)
    lax.fori_loop(0, n // DMA_UNROLL, wait, 0)


def _gather_kernel(nu_ref, tok_ref, u_hbm, o_ref, buf, sem, *, block):
    def row_copy(r):
        return pltpu.make_async_copy(u_hbm.at[pl.ds(tok_ref[0, r], 1)], buf.at[pl.ds(r, 1)], sem)

    @pl.when(pl.program_id(0) < nu_ref[0])
    def _():
        _run_row_copies(block, row_copy)
        o_ref[...] = buf[...].astype(o_ref.dtype)

    @pl.when(pl.program_id(0) >= nu_ref[0])
    def _():
        o_ref[...] = jnp.zeros_like(o_ref)


def gather_rows(u, slot_tok, n_used, *, block=MOE_BLOCK):
    d = u.shape[1]
    n_slots = slot_tok.shape[0]
    return pl.pallas_call(
        functools.partial(_gather_kernel, block=block),
        grid=(n_slots // block,),
        in_specs=[pl.BlockSpec(memory_space=pltpu.SMEM),
                  pl.BlockSpec((None, 1, block), lambda i: (i, 0, 0), memory_space=pltpu.SMEM),
                  pl.BlockSpec(memory_space=pl.ANY)],
        out_specs=pl.BlockSpec((block, d), lambda i: (i, 0)),
        scratch_shapes=[pltpu.VMEM((block, d), F32), pltpu.SemaphoreType.DMA(())],
        out_shape=jax.ShapeDtypeStruct((n_slots, d), BF),
        compiler_params=_params(("arbitrary",)),
        name="moe_gather_rows",
    )(n_used, slot_tok.reshape(n_slots // block, 1, block), u)


def _expert_changed(be_ref, b):
    prev = be_ref[jnp.maximum(b - 1, 0)]
    return jnp.logical_or(b == 0, be_ref[b] != prev)


def _moe1_kernel(be_ref, nu_ref, x_ref, wg_ref, wl_ref, bg_ref, bl_ref, o_ref, wg_bf, wl_bf):
    b = pl.program_id(1)

    @pl.when(_expert_changed(be_ref, b))
    def _():
        wg_bf[...] = wg_ref[...].astype(BF)
        wl_bf[...] = wl_ref[...].astype(BF)

    @pl.when(b < nu_ref[0])
    def _():
        x = x_ref[...]
        hg = jnp.dot(x, wg_bf[...], preferred_element_type=F32) + bg_ref[...]
        hl = jnp.dot(x, wl_bf[...], preferred_element_type=F32) + bl_ref[...]
        glu = jnp.minimum(hg, SWIGLU_LIMIT)
        lin = jnp.clip(hl, -SWIGLU_LIMIT, SWIGLU_LIMIT)
        act = glu * jax.nn.sigmoid(SWIGLU_ALPHA * glu) * (lin + 1.0)
        o_ref[...] = act.astype(o_ref.dtype)

    @pl.when(b >= nu_ref[0])
    def _():
        o_ref[...] = jnp.zeros_like(o_ref)


def moe_up(x_sorted, w1, b1, block_exp, n_used, *, layer, block=MOE_BLOCK, tn=512):
    n_slots, d = x_sorted.shape
    de = w1.shape[-1] // 2
    nb = n_slots // block
    nj = de // tn
    b1r = b1.reshape(b1.shape[0], b1.shape[1], 1, b1.shape[2])
    return pl.pallas_call(
        _moe1_kernel,
        grid_spec=pltpu.PrefetchScalarGridSpec(
            num_scalar_prefetch=2, grid=(nj, nb),
            in_specs=[pl.BlockSpec((block, d), lambda j, b, be, nu: (b, 0)),
                      pl.BlockSpec((None, None, d, tn), lambda j, b, be, nu: (layer, be[b], 0, j)),
                      pl.BlockSpec((None, None, d, tn), lambda j, b, be, nu: (layer, be[b], 0, nj + j)),
                      pl.BlockSpec((None, None, 1, tn), lambda j, b, be, nu: (layer, be[b], 0, j)),
                      pl.BlockSpec((None, None, 1, tn), lambda j, b, be, nu: (layer, be[b], 0, nj + j))],
            out_specs=pl.BlockSpec((block, tn), lambda j, b, be, nu: (b, j)),
            scratch_shapes=[pltpu.VMEM((d, tn), BF), pltpu.VMEM((d, tn), BF)]),
        out_shape=jax.ShapeDtypeStruct((n_slots, de), BF),
        compiler_params=_params(("arbitrary", "arbitrary")),
        name="moe_up",
    )(block_exp, n_used, x_sorted, w1, w1, b1r, b1r)


def _moe2_kernel(be_ref, nu_ref, a_ref, w_ref, b_ref, gate_ref, o_ref, w_bf):
    b = pl.program_id(1)

    @pl.when(_expert_changed(be_ref, b))
    def _():
        w_bf[...] = w_ref[...].astype(BF)

    @pl.when(b < nu_ref[0])
    def _():
        y = jnp.dot(a_ref[...], w_bf[...], preferred_element_type=F32) + b_ref[...]
        o_ref[...] = y * gate_ref[...]

    @pl.when(b >= nu_ref[0])
    def _():
        o_ref[...] = jnp.zeros_like(o_ref)


def moe_down(act, w2, b2, slot_gate, block_exp, n_used, *, layer, block=MOE_BLOCK, tn=1024):
    n_slots, de = act.shape
    d = w2.shape[-1]
    b2r = b2.reshape(b2.shape[0], b2.shape[1], 1, d)
    return pl.pallas_call(
        _moe2_kernel,
        grid_spec=pltpu.PrefetchScalarGridSpec(
            num_scalar_prefetch=2, grid=(d // tn, n_slots // block),
            in_specs=[pl.BlockSpec((block, de), lambda j, b, be, nu: (b, 0)),
                      pl.BlockSpec((None, None, de, tn), lambda j, b, be, nu: (layer, be[b], 0, j)),
                      pl.BlockSpec((None, None, 1, tn), lambda j, b, be, nu: (layer, be[b], 0, j)),
                      pl.BlockSpec((block, 1), lambda j, b, be, nu: (b, 0))],
            out_specs=pl.BlockSpec((block, tn), lambda j, b, be, nu: (b, j)),
            scratch_shapes=[pltpu.VMEM((de, tn), BF)]),
        out_shape=jax.ShapeDtypeStruct((n_slots, d), F32),
        compiler_params=_params(("arbitrary", "arbitrary")),
        name="moe_down",
    )(block_exp, n_used, act, w2, b2r, slot_gate)


def _combine_kernel(dest_ref, y_hbm, o_ref, buf, sem, *, block, top_k):
    def row_copy(n):
        r = n // top_k
        k = n % top_k
        return pltpu.make_async_copy(y_hbm.at[pl.ds(dest_ref[0, n], 1)], buf.at[k, pl.ds(r, 1)], sem)

    _run_row_copies(block * top_k, row_copy)
    acc = buf[0]
    for k in range(1, top_k):
        acc = acc + buf[k]
    o_ref[...] = acc


def combine_rows(y_slots, dest, *, block=128, top_k=TOP_K):
    d = y_slots.shape[1]
    r = dest.shape[0] // top_k
    return pl.pallas_call(
        functools.partial(_combine_kernel, block=block, top_k=top_k),
        grid=(r // block,),
        in_specs=[pl.BlockSpec((None, 1, block * top_k), lambda i: (i, 0, 0), memory_space=pltpu.SMEM),
                  pl.BlockSpec(memory_space=pl.ANY)],
        out_specs=pl.BlockSpec((block, d), lambda i: (i, 0)),
        scratch_shapes=[pltpu.VMEM((top_k, block, d), F32), pltpu.SemaphoreType.DMA(())],
        out_shape=jax.ShapeDtypeStruct((r, d), F32),
        compiler_params=_params(("arbitrary",)),
        name="moe_combine_rows",
    )(dest.reshape(r // block, 1, block * top_k), y_slots)


def moe_dispatch(logits, *, n_tok=N_TOK, n_rows=R_ROWS, block=MOE_BLOCK, n_blocks=N_MOE_BLOCKS):
    top_v, top_i = lax.top_k(logits[:n_tok, :N_EXPERTS], TOP_K)
    gates = jax.nn.softmax(top_v, axis=-1)
    tk = n_tok * TOP_K
    flat_e = top_i.reshape(tk).astype(jnp.int32)
    order = jnp.argsort(flat_e).astype(jnp.int32)
    rank = jnp.argsort(order).astype(jnp.int32)
    experts = jnp.arange(N_EXPERTS, dtype=jnp.int32)
    counts = jnp.sum((flat_e[:, None] == experts[None, :]).astype(jnp.int32), axis=0)
    padded = (counts + block - 1) // block * block
    ends = jnp.cumsum(padded)
    starts = ends - padded
    first = jnp.cumsum(counts) - counts
    zero_slot = (n_blocks - 1) * block
    dest = jnp.concatenate([starts[flat_e] + rank - first[flat_e],
                            jnp.full(((n_rows - n_tok) * TOP_K,), zero_slot, jnp.int32)]).astype(jnp.int32)
    block_start = jnp.arange(n_blocks, dtype=jnp.int32) * block
    block_exp = jnp.minimum(jnp.sum((ends[None, :] <= block_start[:, None]).astype(jnp.int32), axis=1),
                            N_EXPERTS - 1).astype(jnp.int32)
    slot_exp = jnp.repeat(block_exp, block)
    local = jnp.arange(n_blocks * block, dtype=jnp.int32) - starts[slot_exp]
    valid = local < counts[slot_exp]
    pair = order[jnp.clip(first[slot_exp] + local, 0, tk - 1)]
    pad_tok = n_tok
    slot_tok = jnp.where(valid, pair // TOP_K, pad_tok).astype(jnp.int32)
    slot_gate = jnp.where(valid, gates.reshape(tk)[pair], 0.0)
    n_used = (ends[-1] // block).astype(jnp.int32).reshape(1)
    return slot_tok, slot_gate.reshape(-1, 1), block_exp, n_used, dest


def _sample_q_rows(zs, col):
    q = zs[:, col:col + FOX_WIDTH].reshape(DEC_BATCH, DEC_SEQ, FOX_HEADS, HEAD_DIM)
    q = jnp.pad(q, ((0, 0), (0, 0), (0, Q_ROWS_PER_TOKEN - FOX_HEADS), (0, 0)))
    return q.reshape(DEC_BATCH, N_QROWS, HEAD_DIM)


def _sample_new_page(zs, col):
    k = zs[:, col:col + FOX_KV_WIDTH].astype(F32).reshape(DEC_BATCH, DEC_SEQ * FOX_KV_HEADS, HEAD_DIM)
    return jnp.pad(k, ((0, 0), (0, PAGE_COLS - DEC_SEQ * FOX_KV_HEADS), (0, 0)))


def _sample_o_rows(o):
    o = o.reshape(DEC_BATCH, DEC_SEQ, Q_ROWS_PER_TOKEN, HEAD_DIM)[:, :, :FOX_HEADS]
    return o.reshape(N_SAMPLE, FOX_WIDTH)


def kernel(x_prompt, x_sample, cache_fox_k, cache_fox_v, cache_fox_logf, cache_sb_k, cache_sb_v,
           state_hgrn, page_table, c_prompt, c_sample, ada_w, ada_b, norm_mix_g, w_in, fox_fgate_b,
           hgrn_lb_logits, hgrn_gnorm_g, w_branch_a, w_branch_b, w_branch_c, w_out, norm_ffn_g,
           router_w, router_b, moe_w1, moe_b1, moe_w2, moe_b2, final_norm_g):
    d = D_MODEL
    n_pool = cache_fox_k.shape[1]
    lb_sm = jax.nn.softmax(hgrn_lb_logits.astype(F32), axis=0)
    hgrn_lb = jnp.cumsum(lb_sm, axis=0) - lb_sm[0]
    assert w_in.shape[-1] == W_Z1 + W_SHIFT + W_Z2
    w_in_edge = jnp.pad(w_in[..., W_Z1 + W_Z2:], ((0, 0), (0, 0), (0, LANES - W_SHIFT)))
    router_w_p = jnp.pad(router_w, ((0, 0), (0, 0), (0, LANES - N_EXPERTS)))
    router_b_p = jnp.pad(router_b, ((0, 0), (0, LANES - N_EXPERTS))).reshape(DEPTH, 1, LANES)
    cache_views = [c.reshape(DEPTH, n_pool, PAGE_COLS, HEAD_DIM)
                   for c in (cache_fox_k, cache_fox_v, cache_sb_k, cache_sb_v)]
    ck_fox, cv_fox, ck_sb, cv_sb = cache_views

    c_all = jnp.concatenate([c_prompt, c_sample, jnp.zeros((16 - BATCH - DEC_BATCH, d), F32)], axis=0)
    mod = adaln_mod(c_all, ada_w, ada_b)
    mod_p = mod.reshape(DEPTH, 16, 1, N_MOD * d)
    mod_s = jnp.repeat(mod[:, BATCH:BATCH + DEC_BATCH], DEC_SEQ, axis=1)

    x = jnp.concatenate([x_prompt.reshape(N_PROMPT, d), x_sample.reshape(N_SAMPLE, d),
                         jnp.zeros((R_ROWS - N_TOK, d), F32)], axis=0)
    u = norm_rows(x, norm_mix_g, layer=0, mod_p=mod_p, mod_s=mod_s, k_scale=1, k_shift=0)

    outs = {k: [] for k in ("pfk", "pfv", "pfl", "psk", "psv", "ph", "sfk", "sfv", "sfl", "ssk", "ssv", "sh")}
    y = None
    for l in range(DEPTH):
        z1 = matmul(u, w_in, layer=l, tm=MM_TM, tn=MM_TN, out_dtype=BF, n_col_blocks=W_Z1 // MM_TN)
        zf = matmul(u, w_in, layer=l, tm=MM_TM, tn=LANES, out_dtype=F32, col_block0=W_Z1 // LANES,
                    n_col_blocks=1)
        z2 = matmul_shifted(u, w_in, w_in_edge, layer=l, col0=W_Z1, shift=W_SHIFT, width=W_Z2,
                            tm=MM_TM, tn=MM_TN, out_dtype=BF)
        zs1 = z1[N_PROMPT:N_TOK]
        zs2 = z2[N_PROMPT:N_TOK]

        ff = zf[:N_TOK, :FOX_HEADS]
        logf = jax.nn.log_sigmoid(ff + fox_fgate_b[l].astype(F32))
        logf_p = logf[:N_PROMPT].reshape(BATCH, SEQ, FOX_HEADS)
        logf_s = logf[N_PROMPT:].reshape(DEC_BATCH, DEC_SEQ, FOX_HEADS)
        c_p = jnp.cumsum(logf_p, axis=1)
        c_pt = jnp.transpose(c_p, (0, 2, 1))
        past_logf = cache_fox_logf[l][page_table].astype(F32).reshape(DEC_BATCH, PAST_LEN, FOX_HEADS)
        c_s = jnp.cumsum(jnp.concatenate([past_logf, logf_s], axis=1), axis=1)

        oa_p = prompt_attention(z1, kind="fox", c_q=c_pt[..., None],
                                c_k=c_pt.reshape(BATCH, FOX_HEADS, SEQ // 256, 1, 256),
                                col_q=Z1_FQ, col_k=Z1_FK, col_v=Z1_FV, tq=256)
        ob_p = prompt_attention(z2, kind="sb", c_q=None, col_q=Z2_SQ, col_k=Z2_SK, col_v=Z2_SV)
        s0_p = jnp.zeros((BATCH, HGRN_HEADS, HGRN_VDIM, HGRN_EXPAND), F32)
        oc_p, st_p = hgrn_mixer(z2, hgrn_lb, hgrn_gnorm_g, s0_p, layer=l, batch=BATCH, seq=SEQ,
                                col_q=Z2_HQ, col_f=Z2_HF, col_i=Z2_HI, col_g=Z2_HG)

        cq_s = jnp.pad(c_s[:, PAST_LEN:], ((0, 0), (0, 0), (0, Q_ROWS_PER_TOKEN - FOX_HEADS)))
        cq_s = cq_s.reshape(DEC_BATCH, N_QROWS, 1)
        ck_all = jnp.pad(c_s, ((0, 0), (0, PAGE_SIZE - DEC_SEQ), (0, Q_ROWS_PER_TOKEN - FOX_HEADS)))
        ck_all = ck_all.reshape(DEC_BATCH, N_PAGES + 1, PAGE_SIZE, Q_ROWS_PER_TOKEN)
        ck_all = jnp.repeat(jnp.transpose(ck_all, (0, 1, 3, 2)), FOX_KV_HEADS, axis=-1)
        oa_s = sample_attention(_sample_q_rows(zs1, Z1_FQ), ck_fox, cv_fox, _sample_new_page(zs1, Z1_FK),
                                _sample_new_page(zs1, Z1_FV), page_table, kind="fox", layer=l,
                                c_q=cq_s, c_k=ck_all)
        ob_s = sample_attention(_sample_q_rows(zs2, Z2_SQ), ck_sb, cv_sb, _sample_new_page(zs2, Z2_SK),
                                _sample_new_page(zs2, Z2_SV), page_table, kind="sb", layer=l)
        zh = zs2[:, Z2_HQ:Z2_HG + HGRN_WIDTH].reshape(DEC_BATCH, DEC_SEQ, 4 * HGRN_WIDTH)
        zh = jnp.pad(zh, ((0, 0), (0, HGRN_CHUNK - DEC_SEQ), (0, 0))).reshape(DEC_BATCH * HGRN_CHUNK, 4 * HGRN_WIDTH)
        s0_s = jnp.swapaxes(state_hgrn[l].astype(F32), -1, -2)
        oc_s, st_s = hgrn_mixer(zh, hgrn_lb, hgrn_gnorm_g, s0_s, layer=l, batch=DEC_BATCH, seq=HGRN_CHUNK,
                                col_q=0, col_f=HGRN_FDIM, col_i=2 * HGRN_FDIM, col_g=2 * HGRN_FDIM + HGRN_WIDTH,
                                valid_len=DEC_SEQ)
        oc_s = oc_s.reshape(DEC_BATCH, HGRN_CHUNK, HGRN_WIDTH)[:, :DEC_SEQ].reshape(N_SAMPLE, HGRN_WIDTH)

        def rows(p, s):
            return jnp.concatenate([p, s.astype(BF), jnp.zeros((R_ROWS - N_TOK, p.shape[1]), BF)], axis=0)

        o_a = rows(oa_p, _sample_o_rows(oa_s))
        o_b = rows(ob_p, _sample_o_rows(ob_s))
        o_c = rows(oc_p, oc_s)

        merged = merge_branches(o_a, o_b, o_c, w_branch_a, w_branch_b, w_branch_c, z2, layer=l)
        mix = matmul(merged, w_out, layer=l, tm=MM_TM, tn=MM_TN, out_dtype=F32)
        x, u2 = norm_rows(x, norm_ffn_g, layer=l, mod_p=mod_p, mod_s=mod_s, k_scale=4, k_shift=3,
                          delta=mix, k_gate=2, write_x=True, out_dtype=F32)

        logits = router_logits(u2, router_w_p, router_b_p, layer=l)
        slot_tok, slot_gate, block_exp, n_used, dest = moe_dispatch(logits)
        x_sorted = gather_rows(u2, slot_tok, n_used)
        act = moe_up(x_sorted, moe_w1, moe_b1, block_exp, n_used, layer=l)
        y_slots = moe_down(act, moe_w2, moe_b2, slot_gate, block_exp, n_used, layer=l)
        moe_out = combine_rows(y_slots, dest)
        if l + 1 < DEPTH:
            x, u = norm_rows(x, norm_mix_g, layer=l + 1, mod_p=mod_p, mod_s=mod_s, k_scale=1, k_shift=0,
                             delta=moe_out, k_gate=5, gate_layer=l, write_x=True)
        else:
            y = _final_norm(x, final_norm_g, moe_out, mod_p, mod_s, l)

        def kv(z, c0, r0, b, s):
            return z[r0:r0 + b * s, c0:c0 + FOX_KV_WIDTH].astype(F32).reshape(b, s, FOX_KV_HEADS, HEAD_DIM)

        outs["pfk"].append(kv(z1, Z1_FK, 0, BATCH, SEQ))
        outs["pfv"].append(kv(z1, Z1_FV, 0, BATCH, SEQ))
        outs["pfl"].append(logf_p)
        outs["psk"].append(kv(z2, Z2_SK, 0, BATCH, SEQ))
        outs["psv"].append(kv(z2, Z2_SV, 0, BATCH, SEQ))
        outs["ph"].append(jnp.swapaxes(st_p, -1, -2))
        outs["sfk"].append(kv(z1, Z1_FK, N_PROMPT, DEC_BATCH, DEC_SEQ))
        outs["sfv"].append(kv(z1, Z1_FV, N_PROMPT, DEC_BATCH, DEC_SEQ))
        outs["sfl"].append(logf_s)
        outs["ssk"].append(kv(z2, Z2_SK, N_PROMPT, DEC_BATCH, DEC_SEQ))
        outs["ssv"].append(kv(z2, Z2_SV, N_PROMPT, DEC_BATCH, DEC_SEQ))
        outs["sh"].append(jnp.swapaxes(st_s, -1, -2))

    st = {k: jnp.stack(v) for k, v in outs.items()}
    y_prompt = y[:N_PROMPT].reshape(BATCH, SEQ, d)
    y_sample = y[N_PROMPT:N_TOK].reshape(DEC_BATCH, DEC_SEQ, d)
    return (y_prompt, y_sample, st["pfk"], st["pfv"], st["pfl"], st["psk"], st["psv"], st["ph"],
            st["sfk"], st["sfv"], st["sfl"], st["ssk"], st["ssv"], st["sh"])


def _final_norm(x, g, delta, mod_p, mod_s, layer):
    return norm_rows(x, g, layer=layer, mod_p=mod_p, mod_s=mod_s, delta=delta, k_gate=5, out_dtype=F32)
```

```python
import functools

import jax
import jax.numpy as jnp
from jax import lax
from jax.experimental import pallas as pl
from jax.experimental.pallas import tpu as pltpu

BF = jnp.bfloat16
F32 = jnp.float32

D_MODEL = 4096
BATCH = 4
SEQ = 2048
DEPTH = 4
DEC_BATCH = 8
DEC_SEQ = 4
PAST_LEN = 8192
PAGE_SIZE = 128
HEAD_DIM = 128
FOX_HEADS = 12
FOX_KV_HEADS = 4
SB_HEADS = 12
SB_KV_HEADS = 4
GROUP = 3
HGRN_HEADS = 8
HGRN_EXPAND = 128
HGRN_VDIM = 128
HGRN_CHUNK = 64
HGRN_SUB = 16
N_EXPERTS = 32
TOP_K = 4
D_EXPERT = D_MODEL // 4
SWIGLU_LIMIT = 7.0
SWIGLU_ALPHA = 1.702
NORM_EPS = 1e-6
NEG_BIG = -1e30
N_MOD = 6
ATT_SCALE = HEAD_DIM ** -0.5

FOX_WIDTH = FOX_HEADS * HEAD_DIM
FOX_KV_WIDTH = FOX_KV_HEADS * HEAD_DIM
SB_WIDTH = SB_HEADS * HEAD_DIM
SB_KV_WIDTH = SB_KV_HEADS * HEAD_DIM
HGRN_FDIM = HGRN_HEADS * HGRN_EXPAND
HGRN_WIDTH = HGRN_HEADS * HGRN_VDIM
IN_COLS = (FOX_WIDTH, FOX_KV_WIDTH, FOX_KV_WIDTH, FOX_HEADS,
           SB_WIDTH, SB_KV_WIDTH, SB_KV_WIDTH,
           HGRN_FDIM, HGRN_FDIM, HGRN_WIDTH, HGRN_WIDTH,
           D_MODEL, D_MODEL, D_MODEL)

LANES = 128
VMEM_LIMIT_BYTES = 56 * 1024 * 1024

ROW_TILE = 256
N_PROMPT = BATCH * SEQ
N_SAMPLE = DEC_BATCH * DEC_SEQ
N_TOK = N_PROMPT + N_SAMPLE
R_ROWS = N_PROMPT + ROW_TILE
MM_TM = 768
MM_TN = 512

W_Z1 = FOX_WIDTH + 2 * FOX_KV_WIDTH
W_SHIFT = FOX_HEADS
W_Z2 = SB_WIDTH + 2 * SB_KV_WIDTH + 2 * HGRN_FDIM + 2 * HGRN_WIDTH + 3 * D_MODEL
Z1_FQ = 0
Z1_FK = Z1_FQ + FOX_WIDTH
Z1_FV = Z1_FK + FOX_KV_WIDTH
Z2_SQ = 0
Z2_SK = Z2_SQ + SB_WIDTH
Z2_SV = Z2_SK + SB_KV_WIDTH
Z2_HQ = Z2_SV + SB_KV_WIDTH
Z2_HF = Z2_HQ + HGRN_FDIM
Z2_HI = Z2_HF + HGRN_FDIM
Z2_HG = Z2_HI + HGRN_WIDTH
Z2_GA = Z2_HG + HGRN_WIDTH
Z2_GB = Z2_GA + D_MODEL
Z2_GC = Z2_GB + D_MODEL
assert W_Z1 % MM_TN == 0 and W_Z2 % MM_TN == 0 and Z2_GA % MM_TN == 0 and D_MODEL % MM_TN == 0

MOE_BLOCK = 384
N_SLOT_TOK = N_TOK * TOP_K
N_MOE_BLOCKS = -(-N_SLOT_TOK // MOE_BLOCK) + N_EXPERTS
N_SLOTS = N_MOE_BLOCKS * MOE_BLOCK


def _params(sem):
    return pltpu.CompilerParams(dimension_semantics=sem, vmem_limit_bytes=VMEM_LIMIT_BYTES)


def _mm_kernel(x_ref, w_ref, o_ref, wbf_ref):
    @pl.when(pl.program_id(1) == 0)
    def _():
        wbf_ref[...] = w_ref[...].astype(BF)

    o_ref[...] = jnp.dot(x_ref[...], wbf_ref[...], preferred_element_type=F32).astype(o_ref.dtype)


def matmul(x, w, *, layer, tm, tn, out_dtype, col_block0=0, n_col_blocks=None):
    m, k = x.shape
    if n_col_blocks is None:
        n_col_blocks = w.shape[-1] // tn
    assert m % tm == 0
    return pl.pallas_call(
        _mm_kernel,
        grid=(n_col_blocks, m // tm),
        in_specs=[pl.BlockSpec((tm, k), lambda j, i: (i, 0)),
                  pl.BlockSpec((None, k, tn), lambda j, i: (layer, 0, col_block0 + j))],
        out_specs=pl.BlockSpec((tm, tn), lambda j, i: (i, j)),
        out_shape=jax.ShapeDtypeStruct((m, n_col_blocks * tn), out_dtype),
        scratch_shapes=[pltpu.VMEM((k, tn), BF)],
        compiler_params=_params(("arbitrary", "arbitrary")),
        name="dense_matmul",
    )(x, w)


SHIFT_ROWS = 512


def _mm_shift_kernel(x_ref, wa_ref, wb_ref, we_ref, o_ref, wbf_ref, *, shift, tn):
    j = pl.program_id(0)

    @pl.when(pl.program_id(1) == 0)
    def _():
        wide = tn + LANES
        sel = (lax.broadcasted_iota(jnp.int32, (wide, tn), 0)
               == lax.broadcasted_iota(jnp.int32, (wide, tn), 1) + shift).astype(BF)
        lane = lax.broadcasted_iota(jnp.int32, (SHIFT_ROWS, LANES), 1)
        last = j == pl.num_programs(0) - 1
        for r0 in range(0, wbf_ref.shape[0], SHIFT_ROWS):
            rows = pl.ds(r0, SHIFT_ROWS)
            nb = jnp.where(last, we_ref[rows, :], wb_ref[rows, :])
            nb = jnp.where(lane < shift, nb, 0.0)
            cat = jnp.concatenate([wa_ref[rows, :].astype(BF), nb.astype(BF)], axis=1)
            wbf_ref[rows, :] = jnp.dot(cat, sel, preferred_element_type=F32).astype(BF)

    o_ref[...] = jnp.dot(x_ref[...], wbf_ref[...], preferred_element_type=F32).astype(o_ref.dtype)


def matmul_shifted(x, w, w_edge, *, layer, col0, shift, width, tm, tn, out_dtype):
    m, k = x.shape
    n_tiles = width // tn
    assert m % tm == 0 and width % tn == 0 and col0 % tn == 0 and k % SHIFT_ROWS == 0
    assert col0 + shift + width == w.shape[-1]
    wide0 = col0 // tn
    per = tn // LANES
    narrow_last = (col0 + width) // LANES - 1
    return pl.pallas_call(
        functools.partial(_mm_shift_kernel, shift=shift, tn=tn),
        grid=(n_tiles, m // tm),
        in_specs=[pl.BlockSpec((tm, k), lambda j, i: (i, 0)),
                  pl.BlockSpec((None, k, tn), lambda j, i: (layer, 0, wide0 + j)),
                  pl.BlockSpec((None, k, LANES),
                               lambda j, i: (layer, 0, jnp.minimum((wide0 + j + 1) * per, narrow_last))),
                  pl.BlockSpec((None, k, LANES), lambda j, i: (layer, 0, 0))],
        out_specs=pl.BlockSpec((tm, tn), lambda j, i: (i, j)),
        out_shape=jax.ShapeDtypeStruct((m, width), out_dtype),
        scratch_shapes=[pltpu.VMEM((k, tn), BF)],
        compiler_params=_params(("arbitrary", "arbitrary")),
        name="dense_matmul_shifted",
    )(x, w, w, w_edge)


def _ada_kernel(c_ref, w_ref, b_ref, o_ref):
    c = c_ref[...]
    s = (c * jax.nn.sigmoid(c)).astype(BF)
    acc = jnp.dot(s, w_ref[...].astype(BF), preferred_element_type=F32)
    o_ref[...] = acc + b_ref[...]


def adaln_mod(c_all, ada_w, ada_b, *, tn=1024):
    depth, d, n = ada_w.shape
    rows = c_all.shape[0]
    return pl.pallas_call(
        _ada_kernel,
        grid=(depth, n // tn),
        in_specs=[pl.BlockSpec((rows, d), lambda l, j: (0, 0)),
                  pl.BlockSpec((None, d, tn), lambda l, j: (l, 0, j)),
                  pl.BlockSpec((None, 1, tn), lambda l, j: (l, 0, j))],
        out_specs=pl.BlockSpec((None, rows, tn), lambda l, j: (l, 0, j)),
        out_shape=jax.ShapeDtypeStruct((depth, rows, n), F32),
        compiler_params=_params(("arbitrary", "arbitrary")),
        name="adaln_mod",
    )(c_all, ada_w, ada_b.reshape(depth, 1, n))


def _norm_kernel(*refs, has_delta, modulate, write_x, n_prompt_blocks, n_sample):
    it = iter(refs)
    x_ref = next(it)
    if has_delta:
        d_ref, gtp_ref, gts_ref = next(it), next(it), next(it)
    g_ref = next(it)
    if modulate:
        scp_ref, shp_ref, scs_ref, shs_ref = next(it), next(it), next(it), next(it)
    if write_x:
        xo_ref = next(it)
    u_ref = next(it)
    i = pl.program_id(0)

    def body(x, d, gt, sc, sh):
        if has_delta:
            x = x + gt * d
        ms = jnp.mean(x * x, axis=-1, keepdims=True)
        y = x * lax.rsqrt(ms + NORM_EPS) * g_ref[...]
        if modulate:
            y = y * (1.0 + sc) + sh
        return x, y

    @pl.when(i < n_prompt_blocks)
    def _():
        x, y = body(x_ref[...],
                    d_ref[...] if has_delta else None,
                    gtp_ref[...] if has_delta else None,
                    scp_ref[...] if modulate else None,
                    shp_ref[...] if modulate else None)
        if write_x:
            xo_ref[...] = x
        u_ref[...] = y.astype(u_ref.dtype)

    @pl.when(i >= n_prompt_blocks)
    def _():
        rows = pl.ds(0, n_sample)
        x, y = body(x_ref[rows, :],
                    d_ref[rows, :] if has_delta else None,
                    gts_ref[...] if has_delta else None,
                    scs_ref[...] if modulate else None,
                    shs_ref[...] if modulate else None)
        if write_x:
            xo_ref[...] = jnp.zeros_like(xo_ref)
            xo_ref[rows, :] = x
        u_ref[...] = jnp.zeros_like(u_ref)
        u_ref[rows, :] = y.astype(u_ref.dtype)


def norm_rows(x, g, *, layer, mod_p=None, mod_s=None, k_scale=None, k_shift=None,
              delta=None, k_gate=None, gate_layer=None, write_x=False, out_dtype=BF, rows_per_batch=SEQ,
              n_prompt=N_PROMPT, n_sample=N_SAMPLE, tile=ROW_TILE):
    if gate_layer is None:
        gate_layer = layer
    r, d = x.shape
    npb = n_prompt // tile
    bpb = rows_per_batch // tile
    has_delta = delta is not None
    modulate = k_scale is not None
    row_spec = pl.BlockSpec((tile, d), lambda i: (i, 0))

    def pspec(k, lyr=layer):
        return pl.BlockSpec((None, None, 1, d), lambda i: (lyr, jnp.minimum(i, npb - 1) // bpb, 0, k))

    def sspec(k, lyr=layer):
        return pl.BlockSpec((None, n_sample, d), lambda i: (lyr, 0, k))

    args, specs = [x], [row_spec]
    if has_delta:
        args += [delta, mod_p, mod_s]
        specs += [row_spec, pspec(k_gate, gate_layer), sspec(k_gate, gate_layer)]
    if g.ndim == 2:
        args.append(g.reshape(g.shape[0], 1, d))
        specs.append(pl.BlockSpec((None, 1, d), lambda i: (layer, 0, 0)))
    else:
        args.append(g.reshape(1, d))
        specs.append(pl.BlockSpec((1, d), lambda i: (0, 0)))
    if modulate:
        args += [mod_p, mod_p, mod_s, mod_s]
        specs += [pspec(k_scale), pspec(k_shift), sspec(k_scale), sspec(k_shift)]
    out_shape, out_specs = [], []
    if write_x:
        out_shape.append(jax.ShapeDtypeStruct((r, d), F32))
        out_specs.append(row_spec)
    out_shape.append(jax.ShapeDtypeStruct((r, d), out_dtype))
    out_specs.append(row_spec)
    res = pl.pallas_call(
        functools.partial(_norm_kernel, has_delta=has_delta, modulate=modulate, write_x=write_x,
                          n_prompt_blocks=npb, n_sample=n_sample),
        grid=(r // tile,),
        in_specs=specs,
        out_specs=out_specs,
        out_shape=out_shape,
        compiler_params=_params(("arbitrary",)),
        name="resid_norm",
    )(*args)
    return res if write_x else res[0]


ATT_TQ = 512
ATT_TK_SB = 256


def _log_sigmoid(x):
    return jnp.minimum(x, 0.0) - jnp.log1p(jnp.exp(-jnp.abs(x)))


def _split_hi_lo(x):
    hi = x.astype(BF)
    lo = (x - hi.astype(F32)).astype(BF)
    return hi, lo


def _fox_prompt_kernel(q_ref, k_ref, v_ref, cq_ref, ck_ref, o_ref, m_ref, l_ref, acc_ref, *, tq, group):
    qi = pl.program_id(2)
    m_ref[...] = jnp.full_like(m_ref, NEG_BIG)
    l_ref[...] = jnp.zeros_like(l_ref)
    acc_ref[...] = jnp.zeros_like(acc_ref)
    causal = lax.broadcasted_iota(jnp.int32, (tq, tq), 1) <= lax.broadcasted_iota(jnp.int32, (tq, tq), 0)

    def tile(j, diagonal):
        ks = pl.multiple_of(j * tq, tq)
        k = k_ref[pl.ds(ks, tq), :]
        v = v_ref[pl.ds(ks, tq), :]
        for g in range(group):
            q = q_ref[:, g * HEAD_DIM:(g + 1) * HEAD_DIM]
            s = lax.dot_general(q, k, (((1,), (1,)), ((), ())), preferred_element_type=F32)
            s = s * ATT_SCALE + (cq_ref[g] - ck_ref[g, j])
            if diagonal:
                s = jnp.where(causal, s, NEG_BIG)
            m = m_ref[g]
            m_new = jnp.maximum(m, jnp.max(s, axis=-1, keepdims=True))
            a = jnp.exp(m - m_new)
            p = jnp.exp(s - m_new)
            l_ref[g] = a * l_ref[g] + jnp.sum(p, axis=-1, keepdims=True)
            acc_ref[g] = a * acc_ref[g] + jnp.dot(p.astype(BF), v, preferred_element_type=F32)
            m_ref[g] = m_new

    def step(j, carry):
        tile(j, False)
        return carry

    lax.fori_loop(0, qi, step, 0)
    tile(qi, True)
    for g in range(group):
        o_ref[:, g * HEAD_DIM:(g + 1) * HEAD_DIM] = (acc_ref[g] / l_ref[g]).astype(o_ref.dtype)


def _sb_prompt_kernel(q_ref, k_ref, v_ref, o_ref, later_ref, acc_ref, *, tq, tk, group):
    qi = pl.program_id(2)
    ratio = tq // tk
    later_ref[...] = jnp.zeros_like(later_ref)
    acc_ref[...] = jnp.zeros_like(acc_ref)
    r_io = lax.broadcasted_iota(jnp.int32, (tq, tk), 0)
    c_io = lax.broadcasted_iota(jnp.int32, (tq, tk), 1)
    upper = (lax.broadcasted_iota(jnp.int32, (tk, tk), 0) > lax.broadcasted_iota(jnp.int32, (tk, tk), 1)).astype(BF)

    def tile(j, diagonal):
        ks = pl.multiple_of(j * tk, tk)
        k = k_ref[pl.ds(ks, tk), :]
        v = v_ref[pl.ds(ks, tk), :]
        if diagonal:
            strict = c_io + j * tk < r_io + qi * tq
        for g in range(group):
            q = q_ref[:, g * HEAD_DIM:(g + 1) * HEAD_DIM]
            z = lax.dot_general(q, k, (((1,), (1,)), ((), ())), preferred_element_type=F32) * ATT_SCALE
            ln = _log_sigmoid(-z)
            if diagonal:
                ln = jnp.where(strict, ln, 0.0)
            hi, lo = _split_hi_lo(ln)
            after = (jnp.dot(hi, upper, preferred_element_type=F32)
                     + jnp.dot(lo, upper, preferred_element_type=F32)) + later_ref[g]
            a = jnp.exp(z + ln + after)
            if diagonal:
                a = jnp.where(strict, a, 0.0)
            acc_ref[g] += jnp.dot(a.astype(BF), v, preferred_element_type=F32)
            later_ref[g] += jnp.sum(ln, axis=-1, keepdims=True)

    for dd in range(ratio):
        tile(qi * ratio + ratio - 1 - dd, True)

    def step(jj, carry):
        tile(qi * ratio - 1 - jj, False)
        return carry

    lax.fori_loop(0, qi * ratio, step, 0)
    for g in range(group):
        o_ref[:, g * HEAD_DIM:(g + 1) * HEAD_DIM] = acc_ref[g].astype(o_ref.dtype)


def prompt_attention(z, *, kind, c_q, c_k=None, batch=BATCH, seq=SEQ, kv_heads=FOX_KV_HEADS,
                     group=GROUP, col_q, col_k, col_v, tq=ATT_TQ, tk=ATT_TK_SB):
    qw = group * HEAD_DIM
    nq = seq // tq
    assert seq % tq == 0 and tq % tk == 0
    assert col_q % qw == 0 and col_k % HEAD_DIM == 0 and col_v % HEAD_DIM == 0
    in_specs = [
        pl.BlockSpec((tq, qw), lambda b, h, qi: (b * nq + qi, col_q // qw + h)),
        pl.BlockSpec((seq, HEAD_DIM), lambda b, h, qi: (b, col_k // HEAD_DIM + h)),
        pl.BlockSpec((seq, HEAD_DIM), lambda b, h, qi: (b, col_v // HEAD_DIM + h)),
    ]
    args = [z, z, z]
    if kind == "fox":
        in_specs += [pl.BlockSpec((None, group, tq, 1), lambda b, h, qi: (b, h, qi, 0)),
                     pl.BlockSpec((None, group, nq, 1, tq), lambda b, h, qi: (b, h, 0, 0, 0))]
        args += [c_q, c_k]
        kern = functools.partial(_fox_prompt_kernel, tq=tq, group=group)
        scratch = [pltpu.VMEM((group, tq, 1), F32), pltpu.VMEM((group, tq, 1), F32),
                   pltpu.VMEM((group, tq, HEAD_DIM), F32)]
    else:
        kern = functools.partial(_sb_prompt_kernel, tq=tq, tk=tk, group=group)
        scratch = [pltpu.VMEM((group, tq, 1), F32), pltpu.VMEM((group, tq, HEAD_DIM), F32)]
    return pl.pallas_call(
        kern,
        grid=(batch, kv_heads, nq),
        in_specs=in_specs,
        out_specs=pl.BlockSpec((tq, qw), lambda b, h, qi: (b * nq + qi, h)),
        out_shape=jax.ShapeDtypeStruct((batch * seq, kv_heads * qw), BF),
        scratch_shapes=scratch,
        compiler_params=_params(("arbitrary", "arbitrary", "arbitrary")),
        name=kind + "_prompt_attention",
    )(*args)


Q_ROWS_PER_TOKEN = 16
N_QROWS = DEC_SEQ * Q_ROWS_PER_TOKEN
PAGE_COLS = PAGE_SIZE * FOX_KV_HEADS
N_PAGES = PAST_LEN // PAGE_SIZE


def _sample_masks(new_step, strict):
    r = lax.broadcasted_iota(jnp.int32, (N_QROWS, PAGE_COLS), 0)
    c = lax.broadcasted_iota(jnp.int32, (N_QROWS, PAGE_COLS), 1)
    head = r % Q_ROWS_PER_TOKEN
    t = r // Q_ROWS_PER_TOKEN
    pos = c // FOX_KV_HEADS
    ok = (head < FOX_HEADS) & ((c % FOX_KV_HEADS) == head // GROUP)
    causal_new = (pos < t) if strict else (pos <= t)
    return ok & (jnp.logical_not(new_step) | causal_new)


def _fox_sample_kernel(pt_ref, q_ref, kc_ref, vc_ref, kn_ref, vn_ref, cq_ref, ck_ref, o_ref,
                       m_ref, l_ref, acc_ref, *, n_pages):
    p = pl.program_id(1)
    new_step = p == n_pages

    @pl.when(p == 0)
    def _():
        m_ref[...] = jnp.full_like(m_ref, NEG_BIG)
        l_ref[...] = jnp.zeros_like(l_ref)
        acc_ref[...] = jnp.zeros_like(acc_ref)

    k = jnp.where(new_step, kn_ref[...], kc_ref[...]).astype(BF)
    v = jnp.where(new_step, vn_ref[...], vc_ref[...]).astype(BF)
    s = lax.dot_general(q_ref[...], k, (((1,), (1,)), ((), ())), preferred_element_type=F32)
    ck = ck_ref[...]
    bias = cq_ref[...] - jnp.concatenate([ck] * DEC_SEQ, axis=0)
    s = jnp.where(_sample_masks(new_step, False), s * ATT_SCALE + bias, NEG_BIG)
    m = m_ref[...]
    m_new = jnp.maximum(m, jnp.max(s, axis=-1, keepdims=True))
    a = jnp.exp(m - m_new)
    pr = jnp.where(s > 0.5 * NEG_BIG, jnp.exp(s - m_new), 0.0)
    l_ref[...] = a * l_ref[...] + jnp.sum(pr, axis=-1, keepdims=True)
    acc_ref[...] = a * acc_ref[...] + jnp.dot(pr.astype(BF), v, preferred_element_type=F32)
    m_ref[...] = m_new

    @pl.when(new_step)
    def _():
        l = l_ref[...]
        o_ref[...] = acc_ref[...] / jnp.where(l > 0.0, l, 1.0)


def _sb_sample_kernel(pt_ref, q_ref, kc_ref, vc_ref, kn_ref, vn_ref, o_ref, later_ref, acc_ref):
    p = pl.program_id(1)
    new_step = p == 0

    @pl.when(p == 0)
    def _():
        later_ref[...] = jnp.zeros_like(later_ref)
        acc_ref[...] = jnp.zeros_like(acc_ref)

    k = jnp.where(new_step, kn_ref[...], kc_ref[...]).astype(BF)
    v = jnp.where(new_step, vn_ref[...], vc_ref[...]).astype(BF)
    z = lax.dot_general(q_ref[...], k, (((1,), (1,)), ((), ())), preferred_element_type=F32) * ATT_SCALE
    mask = _sample_masks(new_step, True)
    ln = jnp.where(mask, _log_sigmoid(-z), 0.0)
    ci = lax.broadcasted_iota(jnp.int32, (PAGE_COLS, PAGE_COLS), 0) // FOX_KV_HEADS
    cj = lax.broadcasted_iota(jnp.int32, (PAGE_COLS, PAGE_COLS), 1) // FOX_KV_HEADS
    upper = (ci > cj).astype(BF)
    hi, lo = _split_hi_lo(ln)
    after = (jnp.dot(hi, upper, preferred_element_type=F32)
             + jnp.dot(lo, upper, preferred_element_type=F32)) + later_ref[...]
    a = jnp.where(mask, jnp.exp(_log_sigmoid(z) + after), 0.0)
    acc_ref[...] += jnp.dot(a.astype(BF), v, preferred_element_type=F32)
    later_ref[...] += jnp.sum(ln, axis=-1, keepdims=True)

    @pl.when(p == pl.num_programs(1) - 1)
    def _():
        o_ref[...] = acc_ref[...]


def sample_attention(q_rows, cache_k, cache_v, k_new, v_new, page_table, *, kind, layer,
                     c_q=None, c_k=None):
    db = q_rows.shape[0]
    n_pages = page_table.shape[1]
    if kind == "fox":
        def page_of(p, pt, b):
            return pt[b, jnp.minimum(p, n_pages - 1)]
    else:
        def page_of(p, pt, b):
            return pt[b, n_pages - jnp.maximum(p, 1)]
    q_spec = pl.BlockSpec((None, N_QROWS, HEAD_DIM), lambda b, p, pt: (b, 0, 0))
    c_spec = pl.BlockSpec((None, None, PAGE_COLS, HEAD_DIM), lambda b, p, pt: (layer, page_of(p, pt, b), 0, 0))
    n_spec = pl.BlockSpec((None, PAGE_COLS, HEAD_DIM), lambda b, p, pt: (b, 0, 0))
    o_spec = pl.BlockSpec((None, N_QROWS, HEAD_DIM), lambda b, p, pt: (b, 0, 0))
    in_specs = [q_spec, c_spec, c_spec, n_spec, n_spec]
    args = [q_rows, cache_k, cache_v, k_new, v_new]
    if kind == "fox":
        in_specs += [pl.BlockSpec((None, N_QROWS, 1), lambda b, p, pt: (b, 0, 0)),
                     pl.BlockSpec((None, None, Q_ROWS_PER_TOKEN, PAGE_COLS), lambda b, p, pt: (b, p, 0, 0))]
        args += [c_q, c_k]
        kern = functools.partial(_fox_sample_kernel, n_pages=n_pages)
        scratch = [pltpu.VMEM((N_QROWS, 1), F32), pltpu.VMEM((N_QROWS, 1), F32), pltpu.VMEM((N_QROWS, HEAD_DIM), F32)]
    else:
        kern = _sb_sample_kernel
        scratch = [pltpu.VMEM((N_QROWS, 1), F32), pltpu.VMEM((N_QROWS, HEAD_DIM), F32)]
    return pl.pallas_call(
        kern,
        grid_spec=pltpu.PrefetchScalarGridSpec(
            num_scalar_prefetch=1, grid=(db, n_pages + 1), in_specs=in_specs, out_specs=o_spec,
            scratch_shapes=scratch),
        out_shape=jax.ShapeDtypeStruct((db, N_QROWS, HEAD_DIM), F32),
        compiler_params=_params(("arbitrary", "arbitrary")),
        name=kind + "_sample_attention",
    )(page_table, *args)


def _hgrn_kernel(q_ref, f_ref, i_ref, g_ref, lb_ref, gn_ref, s0_ref, o_ref, sfin_ref, st_ref,
                 *, n_chunks, valid_len):
    c, sub = HGRN_CHUNK, HGRN_SUB
    ns = c // sub
    lb = lb_ref[...]
    lb_pos = lb > 0.0
    log_lb = jnp.log(jnp.where(lb_pos, lb, 1.0))
    log_1m = jnp.log1p(-lb)
    tri = (lax.broadcasted_iota(jnp.int32, (c, c), 0) >= lax.broadcasted_iota(jnp.int32, (c, c), 1)).astype(BF)
    blk_col = lax.broadcasted_iota(jnp.int32, (sub, c), 1)
    sub_row = lax.broadcasted_iota(jnp.int32, (sub, 1), 0)
    st_ref[...] = s0_ref[...]

    def chunk(ci, _):
        r0 = pl.multiple_of(ci * c, c)
        x = f_ref[pl.ds(r0, c), :].astype(F32)
        ls = _log_sigmoid(x)
        a = log_1m + ls
        mx = jnp.maximum(log_lb, a)
        lae = mx + jnp.log1p(jnp.exp(-jnp.abs(log_lb - a)))
        logf = jnp.where(lb_pos, lae, ls)
        kk = (1.0 - lb) * jax.nn.sigmoid(-x)
        if valid_len is not None:
            live = (ci * c + lax.broadcasted_iota(jnp.int32, (c, 1), 0)) < valid_len
            logf = jnp.where(live, logf, 0.0)
            kk = jnp.where(live, kk, 0.0)
        qq = q_ref[pl.ds(r0, c), :].astype(F32) * (HGRN_EXPAND ** -0.5)
        vv = i_ref[pl.ds(r0, c), :]
        hi, lo = _split_hi_lo(logf)
        gcum = jnp.dot(tri, hi, preferred_element_type=F32) + jnp.dot(tri, lo, preferred_element_type=F32)
        g_end = gcum[c - 1:c, :]
        st = st_ref[...]
        o = lax.dot_general((qq * jnp.exp(gcum)).astype(BF), st.astype(BF), (((1,), (1,)), ((), ())),
                            preferred_element_type=F32)
        a_rows = []
        for i in range(ns):
            gi = gcum[i * sub:(i + 1) * sub, :]
            qi_ = qq[i * sub:(i + 1) * sub, :]
            ki_ = kk[i * sub:(i + 1) * sub, :]
            if i > 0:
                ref = gcum[i * sub - 1:i * sub, :]
                qt = (qi_ * jnp.exp(gi - ref)).astype(BF)
                kt = (kk * jnp.exp(jnp.minimum(ref - gcum, 0.0))).astype(BF)
                off = lax.dot_general(qt, kt, (((1,), (1,)), ((), ())), preferred_element_type=F32)
                a_i = jnp.where(blk_col < i * sub, off, 0.0)
            else:
                a_i = jnp.zeros((sub, c), F32)
            for s in range(sub):
                dec = jnp.exp(jnp.where(sub_row >= s, gi - gi[s:s + 1, :], 0.0))
                colv = jnp.sum(qi_ * ki_[s:s + 1, :] * dec, axis=-1, keepdims=True)
                a_i = jnp.where((blk_col == i * sub + s) & (sub_row >= s), colv, a_i)
            a_rows.append(a_i)
        amat = jnp.concatenate(a_rows, axis=0)
        o = o + jnp.dot(amat.astype(BF), vv, preferred_element_type=F32)
        kd = (kk * jnp.exp(g_end - gcum)).astype(BF)
        vt = jnp.transpose(vv.astype(F32)).astype(BF)
        st_ref[...] = st * jnp.exp(g_end) + jnp.dot(vt, kd, preferred_element_type=F32)
        o = o * lax.rsqrt(jnp.mean(o * o, axis=-1, keepdims=True) + NORM_EPS)
        hg = g_ref[pl.ds(r0, c), :].astype(F32)
        o = o * gn_ref[...] * (hg * jax.nn.sigmoid(hg))
        o_ref[pl.ds(r0, c), :] = o.astype(o_ref.dtype)
        return 0

    lax.fori_loop(0, n_chunks, chunk, 0)
    sfin_ref[...] = st_ref[...]


def hgrn_mixer(z, lb, gnorm, s0_t, *, layer, batch, seq, col_q, col_f, col_i, col_g, valid_len=None,
               heads=HGRN_HEADS):
    n = HGRN_EXPAND

    def col(c0):
        return pl.BlockSpec((seq, n), lambda b, h: (b, c0 // n + h))

    return pl.pallas_call(
        functools.partial(_hgrn_kernel, n_chunks=seq // HGRN_CHUNK, valid_len=valid_len),
        grid=(batch, heads),
        in_specs=[col(col_q), col(col_f), col(col_i), col(col_g),
                  pl.BlockSpec((None, 1, n), lambda b, h: (layer, 0, h)),
                  pl.BlockSpec((None, 1, n), lambda b, h: (layer, 0, 0)),
                  pl.BlockSpec((None, None, n, n), lambda b, h: (b, h, 0, 0))],
        out_specs=[pl.BlockSpec((seq, n), lambda b, h: (b, h)),
                   pl.BlockSpec((None, None, n, n), lambda b, h: (b, h, 0, 0))],
        out_shape=[jax.ShapeDtypeStruct((batch * seq, heads * n), BF),
                   jax.ShapeDtypeStruct((batch, heads, n, n), F32)],
        scratch_shapes=[pltpu.VMEM((n, n), F32)],
        compiler_params=_params(("arbitrary", "arbitrary")),
        name="hgrn_mixer",
    )(z, z, z, z, lb.reshape(lb.shape[0], 1, -1), gnorm.reshape(gnorm.shape[0], 1, -1), s0_t)


def _merge_kernel(oa_ref, ob_ref, oc_ref, wa_ref, wb_ref, wc_ref, ga_ref, gb_ref, gc_ref, o_ref,
                  wa_bf, wb_bf, wc_bf):
    @pl.when(pl.program_id(1) == 0)
    def _():
        wa_bf[...] = wa_ref[...].astype(BF)
        wb_bf[...] = wb_ref[...].astype(BF)
        wc_bf[...] = wc_ref[...].astype(BF)

    ya = jnp.dot(oa_ref[...], wa_bf[...], preferred_element_type=F32)
    yb = jnp.dot(ob_ref[...], wb_bf[...], preferred_element_type=F32)
    yc = jnp.dot(oc_ref[...], wc_bf[...], preferred_element_type=F32)
    y = (jax.nn.sigmoid(ga_ref[...].astype(F32)) * ya + jax.nn.sigmoid(gb_ref[...].astype(F32)) * yb
         + jax.nn.sigmoid(gc_ref[...].astype(F32)) * yc)
    o_ref[...] = y.astype(o_ref.dtype)


def merge_branches(o_a, o_b, o_c, w_a, w_b, w_c, z, *, layer, tm=MM_TM, tn=MM_TN):
    r = o_a.shape[0]
    d = w_a.shape[-1]

    def lhs(a):
        return pl.BlockSpec((tm, a.shape[1]), lambda j, i: (i, 0))

    def wsp(w):
        return pl.BlockSpec((None, w.shape[1], tn), lambda j, i: (layer, 0, j))

    def gate(c0):
        return pl.BlockSpec((tm, tn), lambda j, i: (i, c0 // tn + j))

    return pl.pallas_call(
        _merge_kernel,
        grid=(d // tn, r // tm),
        in_specs=[lhs(o_a), lhs(o_b), lhs(o_c), wsp(w_a), wsp(w_b), wsp(w_c),
                  gate(Z2_GA), gate(Z2_GB), gate(Z2_GC)],
        out_specs=pl.BlockSpec((tm, tn), lambda j, i: (i, j)),
        out_shape=jax.ShapeDtypeStruct((r, d), BF),
        scratch_shapes=[pltpu.VMEM((w_a.shape[1], tn), BF), pltpu.VMEM((w_b.shape[1], tn), BF),
                        pltpu.VMEM((w_c.shape[1], tn), BF)],
        compiler_params=_params(("arbitrary", "arbitrary")),
        name="merge_branches",
    )(o_a, o_b, o_c, w_a, w_b, w_c, z, z, z)


def _router_kernel(x_ref, w_ref, b_ref, o_ref):
    xh, xl = _split_hi_lo(x_ref[...])
    wh, wl = _split_hi_lo(w_ref[...])
    acc = jnp.dot(xh, wh, preferred_element_type=F32)
    acc = acc + jnp.dot(xh, wl, preferred_element_type=F32)
    acc = acc + jnp.dot(xl, wh, preferred_element_type=F32)
    o_ref[...] = acc + b_ref[...]


def router_logits(u, router_w, router_b, *, layer, tile=ROW_TILE):
    r, d = u.shape
    return pl.pallas_call(
        _router_kernel,
        grid=(r // tile,),
        in_specs=[pl.BlockSpec((tile, d), lambda i: (i, 0)),
                  pl.BlockSpec((None, d, LANES), lambda i: (layer, 0, 0)),
                  pl.BlockSpec((None, 1, LANES), lambda i: (layer, 0, 0))],
        out_specs=pl.BlockSpec((tile, LANES), lambda i: (i, 0)),
        out_shape=jax.ShapeDtypeStruct((r, LANES), F32),
        compiler_params=_params(("arbitrary",)),
        name="router_logits",
    )(u, router_w, router_b)


DMA_UNROLL = 8


def _run_row_copies(n, make_copy):
    assert n % DMA_UNROLL == 0

    def start(i, carry):
        for u in range(DMA_UNROLL):
            make_copy(i * DMA_UNROLL + u).start(priority=u % 2)
        return carry

    def wait(i, carry):
        for u in range(DMA_UNROLL):
            make_copy(i * DMA_UNROLL + u).wait()
        return carry

    lax.fori_loop(0, n // DMA_UNROLL, start, 0)
    lax.fori_loop(0, n // DMA_UNROLL, wait, 0)


def _gather_kernel(nu_ref, tok_ref, u_hbm, o_ref, buf, sem, *, block):
    def row_copy(r):
        return pltpu.make_async_copy(u_hbm.at[pl.ds(tok_ref[0, r], 1)], buf.at[pl.ds(r, 1)], sem)

    @pl.when(pl.program_id(0) < nu_ref[0])
    def _():
        _run_row_copies(block, row_copy)
        o_ref[...] = buf[...].astype(o_ref.dtype)

    @pl.when(pl.program_id(0) >= nu_ref[0])
    def _():
        o_ref[...] = jnp.zeros_like(o_ref)


def gather_rows(u, slot_tok, n_used, *, block=MOE_BLOCK):
    r, d = u.shape
    n_slots = slot_tok.shape[0]
    slab = (d // LANES, LANES)
    out = pl.pallas_call(
        functools.partial(_gather_kernel, block=block),
        grid=(n_slots // block,),
        in_specs=[pl.BlockSpec(memory_space=pltpu.SMEM),
                  pl.BlockSpec((None, 1, block), lambda i: (i, 0, 0), memory_space=pltpu.SMEM),
                  pl.BlockSpec(memory_space=pl.ANY)],
        out_specs=pl.BlockSpec((block,) + slab, lambda i: (i, 0, 0)),
        scratch_shapes=[pltpu.VMEM((block,) + slab, F32), pltpu.SemaphoreType.DMA(())],
        out_shape=jax.ShapeDtypeStruct((n_slots,) + slab, BF),
        compiler_params=_params(("arbitrary",)),
        name="moe_gather_rows",
    )(n_used, slot_tok.reshape(n_slots // block, 1, block), u.reshape((r,) + slab))
    return out.reshape(n_slots, d)


def _expert_changed(be_ref, b):
    prev = be_ref[jnp.maximum(b - 1, 0)]
    return jnp.logical_or(b == 0, be_ref[b] != prev)


def _moe1_kernel(be_ref, nu_ref, x_ref, wg_ref, wl_ref, bg_ref, bl_ref, o_ref, wg_bf, wl_bf):
    b = pl.program_id(1)

    @pl.when(_expert_changed(be_ref, b))
    def _():
        wg_bf[...] = wg_ref[...].astype(BF)
        wl_bf[...] = wl_ref[...].astype(BF)

    @pl.when(b < nu_ref[0])
    def _():
        x = x_ref[...]
        hg = jnp.dot(x, wg_bf[...], preferred_element_type=F32) + bg_ref[...]
        hl = jnp.dot(x, wl_bf[...], preferred_element_type=F32) + bl_ref[...]
        glu = jnp.minimum(hg, SWIGLU_LIMIT)
        lin = jnp.clip(hl, -SWIGLU_LIMIT, SWIGLU_LIMIT)
        act = glu * jax.nn.sigmoid(SWIGLU_ALPHA * glu) * (lin + 1.0)
        o_ref[...] = act.astype(o_ref.dtype)

    @pl.when(b >= nu_ref[0])
    def _():
        o_ref[...] = jnp.zeros_like(o_ref)


def moe_up(x_sorted, w1, b1, block_exp, n_used, *, layer, block=MOE_BLOCK, tn=512):
    n_slots, d = x_sorted.shape
    de = w1.shape[-1] // 2
    nb = n_slots // block
    nj = de // tn
    b1r = b1.reshape(b1.shape[0], b1.shape[1], 1, b1.shape[2])
    return pl.pallas_call(
        _moe1_kernel,
        grid_spec=pltpu.PrefetchScalarGridSpec(
            num_scalar_prefetch=2, grid=(nj, nb),
            in_specs=[pl.BlockSpec((block, d), lambda j, b, be, nu: (b, 0)),
                      pl.BlockSpec((None, None, d, tn), lambda j, b, be, nu: (layer, be[b], 0, j)),
                      pl.BlockSpec((None, None, d, tn), lambda j, b, be, nu: (layer, be[b], 0, nj + j)),
                      pl.BlockSpec((None, None, 1, tn), lambda j, b, be, nu: (layer, be[b], 0, j)),
                      pl.BlockSpec((None, None, 1, tn), lambda j, b, be, nu: (layer, be[b], 0, nj + j))],
            out_specs=pl.BlockSpec((block, tn), lambda j, b, be, nu: (b, j)),
            scratch_shapes=[pltpu.VMEM((d, tn), BF), pltpu.VMEM((d, tn), BF)]),
        out_shape=jax.ShapeDtypeStruct((n_slots, de), BF),
        compiler_params=_params(("arbitrary", "arbitrary")),
        name="moe_up",
    )(block_exp, n_used, x_sorted, w1, w1, b1r, b1r)


def _moe2_kernel(be_ref, nu_ref, a_ref, w_ref, b_ref, gate_ref, o_ref, w_bf):
    b = pl.program_id(1)

    @pl.when(_expert_changed(be_ref, b))
    def _():
        w_bf[...] = w_ref[...].astype(BF)

    @pl.when(b < nu_ref[0])
    def _():
        y = jnp.dot(a_ref[...], w_bf[...], preferred_element_type=F32) + b_ref[...]
        o_ref[...] = y * gate_ref[...]

    @pl.when(b >= nu_ref[0])
    def _():
        o_ref[...] = jnp.zeros_like(o_ref)


def moe_down(act, w2, b2, slot_gate, block_exp, n_used, *, layer, block=MOE_BLOCK, tn=1024):
    n_slots, de = act.shape
    d = w2.shape[-1]
    b2r = b2.reshape(b2.shape[0], b2.shape[1], 1, d)
    return pl.pallas_call(
        _moe2_kernel,
        grid_spec=pltpu.PrefetchScalarGridSpec(
            num_scalar_prefetch=2, grid=(d // tn, n_slots // block),
            in_specs=[pl.BlockSpec((block, de), lambda j, b, be, nu: (b, 0)),
                      pl.BlockSpec((None, None, de, tn), lambda j, b, be, nu: (layer, be[b], 0, j)),
                      pl.BlockSpec((None, None, 1, tn), lambda j, b, be, nu: (layer, be[b], 0, j)),
                      pl.BlockSpec((block, 1), lambda j, b, be, nu: (b, 0))],
            out_specs=pl.BlockSpec((block, tn), lambda j, b, be, nu: (b, j)),
            scratch_shapes=[pltpu.VMEM((de, tn), BF)]),
        out_shape=jax.ShapeDtypeStruct((n_slots, d), F32),
        compiler_params=_params(("arbitrary", "arbitrary")),
        name="moe_down",
    )(block_exp, n_used, act, w2, b2r, slot_gate)


def _combine_kernel(dest_ref, y_hbm, o_ref, buf, sem, *, block, top_k):
    def row_copy(n):
        r = n // top_k
        k = n % top_k
        return pltpu.make_async_copy(y_hbm.at[pl.ds(dest_ref[0, n], 1)], buf.at[k, pl.ds(r, 1)], sem)

    _run_row_copies(block * top_k, row_copy)
    acc = buf[0]
    for k in range(1, top_k):
        acc = acc + buf[k]
    o_ref[...] = acc


def combine_rows(y_slots, dest, *, block=128, top_k=TOP_K):
    d = y_slots.shape[1]
    r = dest.shape[0] // top_k
    return pl.pallas_call(
        functools.partial(_combine_kernel, block=block, top_k=top_k),
        grid=(r // block,),
        in_specs=[pl.BlockSpec((None, 1, block * top_k), lambda i: (i, 0, 0), memory_space=pltpu.SMEM),
                  pl.BlockSpec(memory_space=pl.ANY)],
        out_specs=pl.BlockSpec((block, d), lambda i: (i, 0)),
        scratch_shapes=[pltpu.VMEM((top_k, block, d), F32), pltpu.SemaphoreType.DMA(())],
        out_shape=jax.ShapeDtypeStruct((r, d), F32),
        compiler_params=_params(("arbitrary",)),
        name="moe_combine_rows",
    )(dest.reshape(r // block, 1, block * top_k), y_slots)


def moe_dispatch(logits, *, n_tok=N_TOK, n_rows=R_ROWS, block=MOE_BLOCK, n_blocks=N_MOE_BLOCKS):
    top_v, top_i = lax.top_k(logits[:n_tok, :N_EXPERTS], TOP_K)
    gates = jax.nn.softmax(top_v, axis=-1)
    tk = n_tok * TOP_K
    flat_e = top_i.reshape(tk).astype(jnp.int32)
    order = jnp.argsort(flat_e).astype(jnp.int32)
    rank = jnp.argsort(order).astype(jnp.int32)
    experts = jnp.arange(N_EXPERTS, dtype=jnp.int32)
    counts = jnp.sum((flat_e[:, None] == experts[None, :]).astype(jnp.int32), axis=0)
    padded = (counts + block - 1) // block * block
    ends = jnp.cumsum(padded)
    starts = ends - padded
    first = jnp.cumsum(counts) - counts
    zero_slot = (n_blocks - 1) * block
    dest = jnp.concatenate([starts[flat_e] + rank - first[flat_e],
                            jnp.full(((n_rows - n_tok) * TOP_K,), zero_slot, jnp.int32)]).astype(jnp.int32)
    block_start = jnp.arange(n_blocks, dtype=jnp.int32) * block
    block_exp = jnp.minimum(jnp.sum((ends[None, :] <= block_start[:, None]).astype(jnp.int32), axis=1),
                            N_EXPERTS - 1).astype(jnp.int32)
    local = (jnp.arange(block, dtype=jnp.int32)[None, :] + (block_start - starts[block_exp])[:, None])
    valid = (local < counts[block_exp][:, None]).reshape(-1)
    pair = order[jnp.clip(first[block_exp][:, None] + local, 0, tk - 1).reshape(-1)]
    pad_tok = n_tok
    slot_tok = jnp.where(valid, pair // TOP_K, pad_tok).astype(jnp.int32)
    slot_gate = jnp.where(valid, gates.reshape(tk)[pair], 0.0)
    n_used = (ends[-1] // block).astype(jnp.int32).reshape(1)
    return slot_tok, slot_gate.reshape(-1, 1), block_exp, n_used, dest


def _sample_q_rows(zs, col):
    q = zs[:, col:col + FOX_WIDTH].reshape(DEC_BATCH, DEC_SEQ, FOX_HEADS, HEAD_DIM)
    q = jnp.pad(q, ((0, 0), (0, 0), (0, Q_ROWS_PER_TOKEN - FOX_HEADS), (0, 0)))
    return q.reshape(DEC_BATCH, N_QROWS, HEAD_DIM)


def _sample_new_page(zs, col):
    k = zs[:, col:col + FOX_KV_WIDTH].astype(F32).reshape(DEC_BATCH, DEC_SEQ * FOX_KV_HEADS, HEAD_DIM)
    return jnp.pad(k, ((0, 0), (0, PAGE_COLS - DEC_SEQ * FOX_KV_HEADS), (0, 0)))


def _sample_o_rows(o):
    o = o.reshape(DEC_BATCH, DEC_SEQ, Q_ROWS_PER_TOKEN, HEAD_DIM)[:, :, :FOX_HEADS]
    return o.reshape(N_SAMPLE, FOX_WIDTH)


def kernel(x_prompt, x_sample, cache_fox_k, cache_fox_v, cache_fox_logf, cache_sb_k, cache_sb_v,
           state_hgrn, page_table, c_prompt, c_sample, ada_w, ada_b, norm_mix_g, w_in, fox_fgate_b,
           hgrn_lb_logits, hgrn_gnorm_g, w_branch_a, w_branch_b, w_branch_c, w_out, norm_ffn_g,
           router_w, router_b, moe_w1, moe_b1, moe_w2, moe_b2, final_norm_g):
    d = D_MODEL
    n_pool = cache_fox_k.shape[1]
    lb_sm = jax.nn.softmax(hgrn_lb_logits.astype(F32), axis=0)
    hgrn_lb = jnp.cumsum(lb_sm, axis=0) - lb_sm[0]
    assert w_in.shape[-1] == W_Z1 + W_SHIFT + W_Z2
    w_in_edge = jnp.pad(w_in[..., W_Z1 + W_Z2:], ((0, 0), (0, 0), (0, LANES - W_SHIFT)))
    router_w_p = jnp.pad(router_w, ((0, 0), (0, 0), (0, LANES - N_EXPERTS)))
    router_b_p = jnp.pad(router_b, ((0, 0), (0, LANES - N_EXPERTS))).reshape(DEPTH, 1, LANES)
    cache_views = [c.reshape(DEPTH, n_pool, PAGE_COLS, HEAD_DIM)
                   for c in (cache_fox_k, cache_fox_v, cache_sb_k, cache_sb_v)]
    ck_fox, cv_fox, ck_sb, cv_sb = cache_views

    c_all = jnp.concatenate([c_prompt, c_sample, jnp.zeros((16 - BATCH - DEC_BATCH, d), F32)], axis=0)
    mod = adaln_mod(c_all, ada_w, ada_b)
    mod_p = mod.reshape(DEPTH, 16, 1, N_MOD * d)
    mod_s = jnp.repeat(mod[:, BATCH:BATCH + DEC_BATCH], DEC_SEQ, axis=1)

    x = jnp.concatenate([x_prompt.reshape(N_PROMPT, d), x_sample.reshape(N_SAMPLE, d),
                         jnp.zeros((R_ROWS - N_TOK, d), F32)], axis=0)
    u = norm_rows(x, norm_mix_g, layer=0, mod_p=mod_p, mod_s=mod_s, k_scale=1, k_shift=0)

    outs = {k: [] for k in ("pfk", "pfv", "pfl", "psk", "psv", "ph", "sfk", "sfv", "sfl", "ssk", "ssv", "sh")}
    y = None
    for l in range(DEPTH):
        z1 = matmul(u, w_in, layer=l, tm=MM_TM, tn=MM_TN, out_dtype=BF, n_col_blocks=W_Z1 // MM_TN)
        zf = matmul(u, w_in, layer=l, tm=MM_TM, tn=LANES, out_dtype=F32, col_block0=W_Z1 // LANES,
                    n_col_blocks=1)
        z2 = matmul_shifted(u, w_in, w_in_edge, layer=l, col0=W_Z1, shift=W_SHIFT, width=W_Z2,
                            tm=MM_TM, tn=MM_TN, out_dtype=BF)
        zs1 = z1[N_PROMPT:N_TOK]
        zs2 = z2[N_PROMPT:N_TOK]

        ff = zf[:N_TOK, :FOX_HEADS]
        logf = jax.nn.log_sigmoid(ff + fox_fgate_b[l].astype(F32))
        logf_p = logf[:N_PROMPT].reshape(BATCH, SEQ, FOX_HEADS)
        logf_s = logf[N_PROMPT:].reshape(DEC_BATCH, DEC_SEQ, FOX_HEADS)
        c_p = jnp.cumsum(logf_p, axis=1)
        c_pt = jnp.transpose(c_p, (0, 2, 1))
        past_logf = cache_fox_logf[l][page_table].astype(F32).reshape(DEC_BATCH, PAST_LEN, FOX_HEADS)
        c_s = jnp.cumsum(jnp.concatenate([past_logf, logf_s], axis=1), axis=1)

        oa_p = prompt_attention(z1, kind="fox", c_q=c_pt[..., None],
                                c_k=c_pt.reshape(BATCH, FOX_HEADS, SEQ // ATT_TQ, 1, ATT_TQ),
                                col_q=Z1_FQ, col_k=Z1_FK, col_v=Z1_FV)
        ob_p = prompt_attention(z2, kind="sb", c_q=None, col_q=Z2_SQ, col_k=Z2_SK, col_v=Z2_SV)
        s0_p = jnp.zeros((BATCH, HGRN_HEADS, HGRN_VDIM, HGRN_EXPAND), F32)
        oc_p, st_p = hgrn_mixer(z2, hgrn_lb, hgrn_gnorm_g, s0_p, layer=l, batch=BATCH, seq=SEQ,
                                col_q=Z2_HQ, col_f=Z2_HF, col_i=Z2_HI, col_g=Z2_HG)

        cq_s = jnp.pad(c_s[:, PAST_LEN:], ((0, 0), (0, 0), (0, Q_ROWS_PER_TOKEN - FOX_HEADS)))
        cq_s = cq_s.reshape(DEC_BATCH, N_QROWS, 1)
        ck_all = jnp.pad(c_s, ((0, 0), (0, PAGE_SIZE - DEC_SEQ), (0, Q_ROWS_PER_TOKEN - FOX_HEADS)))
        ck_all = ck_all.reshape(DEC_BATCH, N_PAGES + 1, PAGE_SIZE, Q_ROWS_PER_TOKEN)
        ck_all = jnp.repeat(jnp.transpose(ck_all, (0, 1, 3, 2)), FOX_KV_HEADS, axis=-1)
        oa_s = sample_attention(_sample_q_rows(zs1, Z1_FQ), ck_fox, cv_fox, _sample_new_page(zs1, Z1_FK),
                                _sample_new_page(zs1, Z1_FV), page_table, kind="fox", layer=l,
                                c_q=cq_s, c_k=ck_all)
        ob_s = sample_attention(_sample_q_rows(zs2, Z2_SQ), ck_sb, cv_sb, _sample_new_page(zs2, Z2_SK),
                                _sample_new_page(zs2, Z2_SV), page_table, kind="sb", layer=l)
        zh = zs2[:, Z2_HQ:Z2_HG + HGRN_WIDTH].reshape(DEC_BATCH, DEC_SEQ, 4 * HGRN_WIDTH)
        zh = jnp.pad(zh, ((0, 0), (0, HGRN_CHUNK - DEC_SEQ), (0, 0))).reshape(DEC_BATCH * HGRN_CHUNK, 4 * HGRN_WIDTH)
        s0_s = jnp.swapaxes(state_hgrn[l].astype(F32), -1, -2)
        oc_s, st_s = hgrn_mixer(zh, hgrn_lb, hgrn_gnorm_g, s0_s, layer=l, batch=DEC_BATCH, seq=HGRN_CHUNK,
                                col_q=0, col_f=HGRN_FDIM, col_i=2 * HGRN_FDIM, col_g=2 * HGRN_FDIM + HGRN_WIDTH,
                                valid_len=DEC_SEQ)
        oc_s = oc_s.reshape(DEC_BATCH, HGRN_CHUNK, HGRN_WIDTH)[:, :DEC_SEQ].reshape(N_SAMPLE, HGRN_WIDTH)

        def rows(p, s):
            return jnp.concatenate([p, s.astype(BF), jnp.zeros((R_ROWS - N_TOK, p.shape[1]), BF)], axis=0)

        o_a = rows(oa_p, _sample_o_rows(oa_s))
        o_b = rows(ob_p, _sample_o_rows(ob_s))
        o_c = rows(oc_p, oc_s)

        merged = merge_branches(o_a, o_b, o_c, w_branch_a, w_branch_b, w_branch_c, z2, layer=l)
        mix = matmul(merged, w_out, layer=l, tm=MM_TM, tn=MM_TN, out_dtype=F32)
        x, u2 = norm_rows(x, norm_ffn_g, layer=l, mod_p=mod_p, mod_s=mod_s, k_scale=4, k_shift=3,
                          delta=mix, k_gate=2, write_x=True, out_dtype=F32)

        logits = router_logits(u2, router_w_p, router_b_p, layer=l)
        slot_tok, slot_gate, block_exp, n_used, dest = moe_dispatch(logits)
        x_sorted = gather_rows(u2, slot_tok, n_used)
        act = moe_up(x_sorted, moe_w1, moe_b1, block_exp, n_used, layer=l)
        y_slots = moe_down(act, moe_w2, moe_b2, slot_gate, block_exp, n_used, layer=l)
        moe_out = combine_rows(y_slots, dest)
        if l + 1 < DEPTH:
            x, u = norm_rows(x, norm_mix_g, layer=l + 1, mod_p=mod_p, mod_s=mod_s, k_scale=1, k_shift=0,
                             delta=moe_out, k_gate=5, gate_layer=l, write_x=True)
        else:
            y = _final_norm(x, final_norm_g, moe_out, mod_p, mod_s, l)

        def kv(z, c0, r0, b, s):
            return z[r0:r0 + b * s, c0:c0 + FOX_KV_WIDTH].astype(F32).reshape(b, s, FOX_KV_HEADS, HEAD_DIM)

        outs["pfk"].append(kv(z1, Z1_FK, 0, BATCH, SEQ))
        outs["pfv"].append(kv(z1, Z1_FV, 0, BATCH, SEQ))
        outs["pfl"].append(logf_p)
        outs["psk"].append(kv(z2, Z2_SK, 0, BATCH, SEQ))
        outs["psv"].append(kv(z2, Z2_SV, 0, BATCH, SEQ))
        outs["ph"].append(jnp.swapaxes(st_p, -1, -2))
        outs["sfk"].append(kv(z1, Z1_FK, N_PROMPT, DEC_BATCH, DEC_SEQ))
        outs["sfv"].append(kv(z1, Z1_FV, N_PROMPT, DEC_BATCH, DEC_SEQ))
        outs["sfl"].append(logf_s)
        outs["ssk"].append(kv(z2, Z2_SK, N_PROMPT, DEC_BATCH, DEC_SEQ))
        outs["ssv"].append(kv(z2, Z2_SV, N_PROMPT, DEC_BATCH, DEC_SEQ))
        outs["sh"].append(jnp.swapaxes(st_s, -1, -2))

    st = {k: jnp.stack(v) for k, v in outs.items()}
    y_prompt = y[:N_PROMPT].reshape(BATCH, SEQ, d)
    y_sample = y[N_PROMPT:N_TOK].reshape(DEC_BATCH, DEC_SEQ, d)
    return (y_prompt, y_sample, st["pfk"], st["pfv"], st["pfl"], st["psk"], st["psv"], st["ph"],
            st["sfk"], st["sfv"], st["sfl"], st["ssk"], st["ssv"], st["sh"])


def _final_norm(x, g, delta, mod_p, mod_s, layer):
    return norm_rows(x, g, layer=layer, mod_p=mod_p, mod_s=mod_s, delta=delta, k_gate=5, out_dtype=F32)
```
